```python
import math
import jax, jax.numpy as jnp
from jax import lax
import numpy as np

D_MODEL = 2048
BATCH = 1
SEQ = 8192
DEPTH = 1

N_HEADS_A = 8
HEAD_DIM_A = 128
N_IDX_HEADS = 16
IDX_DIM = 64
TOPK_MAX = 256
N_HEADS_B = 8
DIFF_DIM = 64
N_MEM = 256
N_MEM_HEADS = 4
MEM_HEAD_DIM = 256
N_BRANCH = 3

WIDTH_A = N_HEADS_A * HEAD_DIM_A
WIDTH_B = N_HEADS_B * 2 * DIFF_DIM
WIDTH_M = N_MEM_HEADS * MEM_HEAD_DIM
Q_BLOCK = 128
RMS_EPS = 1e-6
NEG_INF = -1e30

SPLITS = (
    WIDTH_A, WIDTH_A, WIDTH_A, WIDTH_A,
    N_IDX_HEADS * IDX_DIM, IDX_DIM, N_IDX_HEADS,
    WIDTH_B, WIDTH_B, WIDTH_B, WIDTH_B,
    WIDTH_M, WIDTH_M,
    N_BRANCH * D_MODEL,
)
D_IN = sum(SPLITS)

kernel_name = "hybrid_dsa_diffattn_memory_gated_block"


def rms_norm(x, g):
    xf = x.astype(jnp.float32)
    y = xf * lax.rsqrt(jnp.mean(xf * xf, axis=-1, keepdims=True) + RMS_EPS)
    return (y * g.astype(jnp.float32)).astype(x.dtype)


def alibi_slopes(n):
    return 2.0 ** (-8.0 * jnp.arange(1, n + 1, dtype=jnp.float32) / n)


def to_blocks(a):
    b, s = a.shape[:2]
    return jnp.moveaxis(a.reshape(b, s // Q_BLOCK, Q_BLOCK, *a.shape[2:]), 1, 0)


def from_blocks(a):
    nb, b, q = a.shape[:3]
    return jnp.moveaxis(a, 0, 1).reshape(b, nb * q, *a.shape[3:])


def dsa_attention(q, k, v, q_idx, k_idx, w_idx, topk):
    s_len = k.shape[1]
    slopes = alibi_slopes(N_HEADS_A)
    key_pos = jnp.arange(s_len, dtype=jnp.int32)
    qpos_blocks = key_pos.reshape(-1, Q_BLOCK)
    scale = HEAD_DIM_A ** -0.5
    idx_scale = IDX_DIM ** -0.5
    gather = jax.vmap(lambda a, i: a[i])

    def block(args):
        qb, qib, wb, tpos = args
        s_h = jnp.einsum('bqjd,bsd->bqjs', qib, k_idx).astype(jnp.float32) * idx_scale
        score = jnp.einsum('bqjs,bqj->bqs', jax.nn.relu(s_h), wb.astype(jnp.float32))
        causal = key_pos[None, :] <= tpos[:, None]
        score = jnp.where(causal[None], score, -jnp.inf)
        _, sel = lax.top_k(score, topk)
        valid = sel <= tpos[None, :, None]
        k_sel = gather(k, sel)
        v_sel = gather(v, sel)
        logits = jnp.einsum('bqhd,bqkhd->bhqk', qb, k_sel).astype(jnp.float32) * scale
        dist = (tpos[None, :, None] - sel).astype(jnp.float32)
        logits = logits - slopes[None, :, None, None] * dist[:, None]
        logits = jnp.where(valid[:, None], logits, NEG_INF)
        p = jax.nn.softmax(logits, axis=-1).astype(v.dtype)
        return jnp.einsum('bhqk,bqkhd->bqhd', p, v_sel)

    out = lax.map(block, (to_blocks(q), to_blocks(q_idx), to_blocks(w_idx), qpos_blocks))
    return from_blocks(out)


def diff_attention(q, k, v, lam, lam_init, g_subln):
    s_len = k.shape[1]
    slopes = alibi_slopes(N_HEADS_B)
    key_pos = jnp.arange(s_len, dtype=jnp.int32)
    qpos_blocks = key_pos.reshape(-1, Q_BLOCK)
    scale = DIFF_DIM ** -0.5

    def block(args):
        qb, tpos = args
        logits = jnp.einsum('bqhcd,bshcd->bhcqs', qb, k).astype(jnp.float32) * scale
        dist = (tpos[:, None] - key_pos[None, :]).astype(jnp.float32)
        logits = logits - slopes[None, :, None, None, None] * dist
        logits = jnp.where(dist >= 0, logits, NEG_INF)
        p = jax.nn.softmax(logits, axis=-1)
        a = p[:, :, 0] - lam * p[:, :, 1]
        return jnp.einsum('bhqs,bshe->bqhe', a.astype(v.dtype), v)

    out = from_blocks(lax.map(block, (to_blocks(q), qpos_blocks)))
    return rms_norm(out, g_subln) * (1.0 - lam_init)


def memory_attention(q, k, v):
    logits = jnp.einsum('bshd,bmhd->bhsm', q, k).astype(jnp.float32) * (MEM_HEAD_DIM ** -0.5)
    p = jax.nn.softmax(logits, axis=-1).astype(v.dtype)
    return jnp.einsum('bhsm,bmhd->bshd', p, v)


def setup_inputs(seed: int = 0) -> dict:
    key = jax.random.key(seed)
    ks = jax.random.split(key, 18)
    f32 = jnp.float32

    def nrm(k, shape, scale):
        return jax.random.normal(k, shape, f32) * scale

    return {
        "x": nrm(ks[0], (BATCH, SEQ, D_MODEL), 1.0),
        "mem": nrm(ks[1], (BATCH, N_MEM, D_MODEL), 1.0),
        "g_in": 1.0 + nrm(ks[2], (DEPTH, D_MODEL), 0.02),
        "w_in": nrm(ks[3], (DEPTH, D_MODEL, D_IN), D_MODEL ** -0.5),
        "b_gate": nrm(ks[4], (DEPTH, N_BRANCH * D_MODEL), 0.02),
        "lam_q1": nrm(ks[5], (DEPTH, DIFF_DIM), 0.1),
        "lam_k1": nrm(ks[6], (DEPTH, DIFF_DIM), 0.1),
        "lam_q2": nrm(ks[7], (DEPTH, DIFF_DIM), 0.1),
        "lam_k2": nrm(ks[8], (DEPTH, DIFF_DIM), 0.1),
        "g_subln": 1.0 + nrm(ks[9], (DEPTH, 2 * DIFF_DIM), 0.02),
        "g_mem": 1.0 + nrm(ks[10], (DEPTH, D_MODEL), 0.02),
        "w_mem_kv": nrm(ks[11], (DEPTH, D_MODEL, 2 * WIDTH_M), D_MODEL ** -0.5),
        "w_br_a": nrm(ks[12], (DEPTH, WIDTH_A, D_MODEL), WIDTH_A ** -0.5),
        "w_br_b": nrm(ks[13], (DEPTH, WIDTH_B, D_MODEL), WIDTH_B ** -0.5),
        "w_br_m": nrm(ks[14], (DEPTH, WIDTH_M, D_MODEL), WIDTH_M ** -0.5),
        "w_out": nrm(ks[15], (DEPTH, D_MODEL, D_MODEL), D_MODEL ** -0.5),
        "g_final": 1.0 + nrm(ks[16], (D_MODEL,), 0.02),
    }


def reference(x, mem, g_in, w_in, b_gate, lam_q1, lam_k1, lam_q2, lam_k2, g_subln,
              g_mem, w_mem_kv, w_br_a, w_br_b, w_br_m, w_out, g_final):
    bsz, s_len, _ = x.shape
    topk = min(TOPK_MAX, s_len // 4)
    split_points = np.cumsum(SPLITS)[:-1].tolist()
    for l in range(DEPTH):
        h = rms_norm(x, g_in[l])
        proj = h @ w_in[l]
        (qa, ka, va, za, qi, ki, wi, qb, kb, vb, zb, qm, zm, gl) = jnp.split(proj, split_points, axis=-1)

        ya = dsa_attention(
            qa.reshape(bsz, s_len, N_HEADS_A, HEAD_DIM_A),
            ka.reshape(bsz, s_len, N_HEADS_A, HEAD_DIM_A),
            va.reshape(bsz, s_len, N_HEADS_A, HEAD_DIM_A),
            qi.reshape(bsz, s_len, N_IDX_HEADS, IDX_DIM),
            ki,
            wi * (N_IDX_HEADS ** -0.5),
            topk,
        )
        ua = ya.reshape(bsz, s_len, WIDTH_A) * jax.nn.silu(za)

        lam_init = 0.8 - 0.6 * math.exp(-0.3 * l)
        lam = (jnp.exp(jnp.sum(lam_q1[l].astype(jnp.float32) * lam_k1[l].astype(jnp.float32)))
               - jnp.exp(jnp.sum(lam_q2[l].astype(jnp.float32) * lam_k2[l].astype(jnp.float32)))
               + lam_init)
        yb = diff_attention(
            qb.reshape(bsz, s_len, N_HEADS_B, 2, DIFF_DIM),
            kb.reshape(bsz, s_len, N_HEADS_B, 2, DIFF_DIM),
            vb.reshape(bsz, s_len, N_HEADS_B, 2 * DIFF_DIM),
            lam, lam_init, g_subln[l],
        )
        ub = yb.reshape(bsz, s_len, WIDTH_B) * jax.nn.silu(zb)

        mkv = rms_norm(mem, g_mem[l]) @ w_mem_kv[l]
        km, vm = jnp.split(mkv, 2, axis=-1)
        n_mem = mem.shape[1]
        ym = memory_attention(
            qm.reshape(bsz, s_len, N_MEM_HEADS, MEM_HEAD_DIM),
            km.reshape(bsz, n_mem, N_MEM_HEADS, MEM_HEAD_DIM),
            vm.reshape(bsz, n_mem, N_MEM_HEADS, MEM_HEAD_DIM),
        )
        um = ym.reshape(bsz, s_len, WIDTH_M) * jax.nn.silu(zm)

        gates = jax.nn.sigmoid((gl + b_gate[l]).astype(jnp.float32)).astype(x.dtype)
        gates = gates.reshape(bsz, s_len, N_BRANCH, D_MODEL)
        merged = (gates[:, :, 0] * (ua @ w_br_a[l])
                  + gates[:, :, 1] * (ub @ w_br_b[l])
                  + gates[:, :, 2] * (um @ w_br_m[l]))
        x = x + merged @ w_out[l]
    return rms_norm(x, g_final)
```

```python
import functools

import jax
import jax.numpy as jnp
from jax import lax
from jax.experimental import pallas as pl
from jax.experimental.pallas import tpu as pltpu

F32 = jnp.float32
CDT = jnp.bfloat16

D_MODEL = 2048
N_HEADS_A = 8
HEAD_DIM_A = 128
N_IDX_HEADS = 16
IDX_DIM = 64
TOPK_MAX = 256
N_HEADS_B = 8
DIFF_DIM = 64
N_MEM_HEADS = 4
MEM_HEAD_DIM = 256
N_BRANCH = 3
WIDTH = 1024
RMS_EPS = 1e-6
NEG_INF = -1e30
F32_BIG = 3.0e38
LAM_INIT = 0.2
LANES = 128

C_GL = 0
C_QI = 6144
C_QA = 7168
C_KA = 8192
C_VA = 9216
C_ZA = 10240
C_QB = 11264
C_KB = 12288
C_VB = 13312
C_ZB = 14336
C_QM = 15360
C_ZM = 16384
C_KK = 17408
C_WI = 17536
D_PAD = 17664

VMEM_LIMIT = 56 * 1024 * 1024

BISECT_MAX_ITERS = 64


def _nt_dot(a, b):
    return lax.dot_general(a, b, (((1,), (1,)), ((), ())), preferred_element_type=F32)


def _silu(z):
    return z * jax.nn.sigmoid(z)


def _proj_kernel(x_ref, g_ref, w_ref, o_ref, h_ref):
    @pl.when(pl.program_id(1) == 0)
    def _():
        xf = x_ref[...]
        ms = jnp.mean(xf * xf, axis=-1, keepdims=True)
        h_ref[...] = (xf * lax.rsqrt(ms + RMS_EPS) * g_ref[...]).astype(h_ref.dtype)

    o_ref[...] = jnp.dot(h_ref[...], w_ref[...], preferred_element_type=F32).astype(o_ref.dtype)


def _proj(x2, g, w, tm, tn):
    s, d = x2.shape
    n = w.shape[1]
    return pl.pallas_call(
        _proj_kernel,
        out_shape=jax.ShapeDtypeStruct((s, n), CDT),
        grid=(s // tm, n // tn),
        in_specs=[
            pl.BlockSpec((tm, d), lambda i, j: (i, 0)),
            pl.BlockSpec((1, d), lambda i, j: (0, 0)),
            pl.BlockSpec((d, tn), lambda i, j: (0, j)),
        ],
        out_specs=pl.BlockSpec((tm, tn), lambda i, j: (i, j)),
        scratch_shapes=[pltpu.VMEM((tm, d), CDT)],
        compiler_params=pltpu.CompilerParams(
            dimension_semantics=("parallel", "arbitrary"), vmem_limit_bytes=VMEM_LIMIT),
        name="proj",
    )(x2, g, w)


def _dsa_kernel(slopes_ref, qa_ref, ka_ref, va_ref, za_ref, qi_ref, kk_ref, wi_ref, o_ref,
                sc_ref, qz_ref, wb_ref, m_ref, l_ref, acc_ref, *, tq, topk):
    qi = pl.program_id(0)
    h = pl.program_id(1)
    q0 = qi * tq
    row = lax.broadcasted_iota(jnp.int32, (tq, tq), 0)
    col = lax.broadcasted_iota(jnp.int32, (tq, tq), 1)
    nsub = tq // LANES

    def fold(a, op):
        r = a[:, :LANES]
        for t in range(1, nsub):
            r = op(r, a[:, t * LANES:(t + 1) * LANES])
        return r

    @pl.when(h == 0)
    def _select():
        lane = lax.broadcasted_iota(jnp.int32, (tq, LANES), 1)
        for m in range(N_IDX_HEADS // 2):
            qp = qi_ref[:, m * LANES:(m + 1) * LANES].astype(F32)
            qz_ref[2 * m] = jnp.where(lane < IDX_DIM, qp, 0.0).astype(qz_ref.dtype)
            qz_ref[2 * m + 1] = jnp.where(lane >= IDX_DIM, qp, 0.0).astype(qz_ref.dtype)
        wi = wi_ref[...].astype(F32) * (IDX_DIM ** -0.5 * N_IDX_HEADS ** -0.5)
        for j in range(N_IDX_HEADS):
            wb_ref[j] = jnp.broadcast_to(wi[:, j:j + 1], (tq, LANES))

        def chunk_scores(kc):
            kk = kk_ref[pl.ds(pl.multiple_of(kc * tq, tq), tq), :]
            acc = jnp.zeros((tq, tq), F32)
            for j in range(N_IDX_HEADS):
                s = _nt_dot(qz_ref[j], kk)
                w = wb_ref[j]
                acc = acc + jnp.maximum(s, 0.0) * jnp.concatenate([w] * nsub, axis=1)
            return acc

        def full_body(kc, carry):
            rmin, rmax = carry
            a = chunk_scores(kc)
            sc_ref[kc] = a
            return jnp.minimum(rmin, fold(a, jnp.minimum)), jnp.maximum(rmax, fold(a, jnp.maximum))

        rmin, rmax = lax.fori_loop(
            0, qi, full_body,
            (jnp.full((tq, LANES), F32_BIG, F32), jnp.full((tq, LANES), -F32_BIG, F32)))
        a = chunk_scores(qi)
        causal = col <= row
        sc_ref[qi] = jnp.where(causal, a, -F32_BIG)
        rmin = jnp.minimum(rmin, fold(jnp.where(causal, a, F32_BIG), jnp.minimum))
        rmax = jnp.maximum(rmax, fold(jnp.where(causal, a, -F32_BIG), jnp.maximum))
        lo0 = jnp.min(rmin, axis=1, keepdims=True)
        hi0 = jnp.max(rmax, axis=1, keepdims=True)

        kf = float(topk)
        n_causal = (q0 + row[:, :1] + 1).astype(F32)

        def cond(c):
            return jnp.logical_and(c[0] < BISECT_MAX_ITERS, c[4])

        def body(c):
            it, lo, hi, cnt_lo, _ = c
            mid = lo + 0.5 * (hi - lo)
            midb = jnp.broadcast_to(mid, (tq, tq))

            def cbody(kc, cnt):
                ind = jnp.where(sc_ref[kc] >= midb, 1.0, 0.0)
                return cnt + fold(ind, jnp.add)

            cnt = lax.fori_loop(0, qi + 1, cbody, jnp.zeros((tq, LANES), F32))
            cm = jnp.sum(cnt, axis=1, keepdims=True)
            ge = cm >= kf
            moving = jnp.logical_and(mid > lo, mid < hi)
            lo = jnp.where(ge, mid, lo)
            hi = jnp.where(ge, hi, mid)
            cnt_lo = jnp.where(ge, cm, cnt_lo)
            active = jnp.logical_and(cnt_lo > kf, moving)
            flag = jnp.max(jnp.where(active, 1.0, 0.0)) > 0.0
            return it + 1, lo, hi, cnt_lo, flag

        c0 = (jnp.int32(0), lo0, hi0, n_causal, jnp.max(n_causal) > kf)
        _, lo, _, _, _ = lax.while_loop(cond, body, c0)
        lob = jnp.broadcast_to(lo, (tq, tq))

        def mbody(kc, _):
            sc_ref[kc] = jnp.where(sc_ref[kc] >= lob, 0.0, NEG_INF)
            return 0

        lax.fori_loop(0, qi + 1, mbody, 0)

    slope = slopes_ref[h]
    q = qa_ref[...]
    m_ref[...] = jnp.full(m_ref.shape, NEG_INF, F32)
    l_ref[...] = jnp.zeros(l_ref.shape, F32)
    acc_ref[...] = jnp.zeros(acc_ref.shape, F32)
    rc = (row - col).astype(F32)

    def abody(kc, _):
        ks = pl.ds(pl.multiple_of(kc * tq, tq), tq)
        k = ka_ref[ks, :]
        v = va_ref[ks, :]
        off = (q0 - kc * tq).astype(F32)
        s = _nt_dot(q, k) * (HEAD_DIM_A ** -0.5) - slope * (rc + off) + sc_ref[kc]
        m_prev = m_ref[...]
        m_new = jnp.maximum(m_prev, jnp.max(s, axis=1, keepdims=True))
        alpha = jnp.exp(m_prev - m_new)
        p = jnp.exp(s - m_new)
        l_ref[...] = alpha * l_ref[...] + jnp.sum(p, axis=1, keepdims=True)
        acc_ref[...] = alpha * acc_ref[...] + jnp.dot(p.astype(v.dtype), v, preferred_element_type=F32)
        m_ref[...] = m_new
        return 0

    lax.fori_loop(0, qi + 1, abody, 0)
    y = acc_ref[...] / l_ref[...]
    o_ref[...] = (y * _silu(za_ref[...].astype(F32))).astype(o_ref.dtype)


def _dsa(p, slopes, tq, topk):
    s = p.shape[0]
    nq = s // tq
    cb = lambda c: c // LANES
    kern = functools.partial(_dsa_kernel, tq=tq, topk=topk)
    return pl.pallas_call(
        kern,
        out_shape=jax.ShapeDtypeStruct((s, WIDTH), CDT),
        grid=(nq, N_HEADS_A),
        in_specs=[
            pl.BlockSpec(memory_space=pltpu.SMEM),
            pl.BlockSpec((tq, LANES), lambda i, h: (i, cb(C_QA) + h)),
            pl.BlockSpec((s, LANES), lambda i, h: (0, cb(C_KA) + h)),
            pl.BlockSpec((s, LANES), lambda i, h: (0, cb(C_VA) + h)),
            pl.BlockSpec((tq, LANES), lambda i, h: (i, cb(C_ZA) + h)),
            pl.BlockSpec((tq, WIDTH), lambda i, h: (i, C_QI // WIDTH)),
            pl.BlockSpec((s, LANES), lambda i, h: (0, cb(C_KK))),
            pl.BlockSpec((tq, LANES), lambda i, h: (i, cb(C_WI))),
        ],
        out_specs=pl.BlockSpec((tq, LANES), lambda i, h: (i, h)),
        scratch_shapes=[
            pltpu.VMEM((nq, tq, tq), F32),
            pltpu.VMEM((N_IDX_HEADS, tq, LANES), CDT),
            pltpu.VMEM((N_IDX_HEADS, tq, LANES), F32),
            pltpu.VMEM((tq, 1), F32),
            pltpu.VMEM((tq, 1), F32),
            pltpu.VMEM((tq, LANES), F32),
        ],
        compiler_params=pltpu.CompilerParams(
            dimension_semantics=("parallel", "arbitrary"), vmem_limit_bytes=VMEM_LIMIT),
        name="dsa",
    )(slopes, p, p, p, p, p, p, p)


def _diff_kernel(slopes_ref, lam_ref, qb_ref, kb_ref, vb_ref, zb_ref, g_ref, o_ref,
                 m_ref, l_ref, acc_ref, *, tq):
    h = pl.program_id(0)
    qi = pl.program_id(1)
    slope = slopes_ref[h]
    row = lax.broadcasted_iota(jnp.int32, (tq, tq), 0)
    col = lax.broadcasted_iota(jnp.int32, (tq, tq), 1)
    rc = (row - col).astype(F32)
    lane = lax.broadcasted_iota(jnp.int32, (tq, LANES), 1)
    qf = qb_ref[...].astype(F32)
    qs = (jnp.where(lane < DIFF_DIM, qf, 0.0).astype(qb_ref.dtype),
          jnp.where(lane >= DIFF_DIM, qf, 0.0).astype(qb_ref.dtype))
    m_ref[...] = jnp.full(m_ref.shape, NEG_INF, F32)
    l_ref[...] = jnp.zeros(l_ref.shape, F32)
    acc_ref[...] = jnp.zeros(acc_ref.shape, F32)

    def step(kc, diag):
        ks = pl.ds(pl.multiple_of(kc * tq, tq), tq)
        k = kb_ref[ks, :]
        v = vb_ref[ks, :]
        off = ((qi - kc) * tq).astype(F32)
        bias = -slope * (rc + off)
        for c in range(2):
            s = _nt_dot(qs[c], k) * (DIFF_DIM ** -0.5) + bias
            if diag:
                s = jnp.where(rc >= 0.0, s, NEG_INF)
            m_prev = m_ref[c]
            m_new = jnp.maximum(m_prev, jnp.max(s, axis=1, keepdims=True))
            alpha = jnp.exp(m_prev - m_new)
            p = jnp.exp(s - m_new)
            l_ref[c] = alpha * l_ref[c] + jnp.sum(p, axis=1, keepdims=True)
            acc_ref[c] = alpha * acc_ref[c] + jnp.dot(p.astype(v.dtype), v, preferred_element_type=F32)
            m_ref[c] = m_new

    def body(kc, _):
        step(kc, False)
        return 0

    lax.fori_loop(0, qi, body, 0)
    step(qi, True)

    lv = lam_ref[...]
    lam = (jnp.exp(jnp.sum(lv[0:1] * lv[1:2], axis=1, keepdims=True))
           - jnp.exp(jnp.sum(lv[2:3] * lv[3:4], axis=1, keepdims=True)) + LAM_INIT)
    y = acc_ref[0] / l_ref[0] - lam * (acc_ref[1] / l_ref[1])
    ms = jnp.mean(y * y, axis=-1, keepdims=True)
    y = (y * lax.rsqrt(ms + RMS_EPS) * g_ref[...]) * (1.0 - LAM_INIT)
    o_ref[...] = (y * _silu(zb_ref[...].astype(F32))).astype(o_ref.dtype)


def _diff(p, slopes, lamv, g_subln, tq):
    s = p.shape[0]
    nq = s // tq
    cb = lambda c: c // LANES
    kern = functools.partial(_diff_kernel, tq=tq)
    return pl.pallas_call(
        kern,
        out_shape=jax.ShapeDtypeStruct((s, WIDTH), CDT),
        grid=(N_HEADS_B, nq),
        in_specs=[
            pl.BlockSpec(memory_space=pltpu.SMEM),
            pl.BlockSpec((4, DIFF_DIM), lambda h, i: (0, 0)),
            pl.BlockSpec((tq, LANES), lambda h, i: (i, cb(C_QB) + h)),
            pl.BlockSpec((s, LANES), lambda h, i: (0, cb(C_KB) + h)),
            pl.BlockSpec((s, LANES), lambda h, i: (0, cb(C_VB) + h)),
            pl.BlockSpec((tq, LANES), lambda h, i: (i, cb(C_ZB) + h)),
            pl.BlockSpec((1, LANES), lambda h, i: (0, 0)),
        ],
        out_specs=pl.BlockSpec((tq, LANES), lambda h, i: (i, h)),
        scratch_shapes=[
            pltpu.VMEM((2, tq, 1), F32),
            pltpu.VMEM((2, tq, 1), F32),
            pltpu.VMEM((2, tq, LANES), F32),
        ],
        compiler_params=pltpu.CompilerParams(
            dimension_semantics=("parallel", "parallel"), vmem_limit_bytes=VMEM_LIMIT),
        name="diff",
    )(slopes, lamv, p, p, p, p, g_subln)


def _memkv_kernel(mem_ref, g_ref, w_ref, o_ref, h_ref):
    @pl.when(pl.program_id(0) == 0)
    def _():
        xf = mem_ref[...]
        ms = jnp.mean(xf * xf, axis=-1, keepdims=True)
        h_ref[...] = (xf * lax.rsqrt(ms + RMS_EPS) * g_ref[...]).astype(h_ref.dtype)

    o_ref[...] = jnp.dot(h_ref[...], w_ref[...].astype(h_ref.dtype),
                         preferred_element_type=F32).astype(o_ref.dtype)


def _memkv(mem2, g, w, tn):
    nm, d = mem2.shape
    n = w.shape[1]
    return pl.pallas_call(
        _memkv_kernel,
        out_shape=jax.ShapeDtypeStruct((nm, n), CDT),
        grid=(n // tn,),
        in_specs=[
            pl.BlockSpec((nm, d), lambda j: (0, 0)),
            pl.BlockSpec((1, d), lambda j: (0, 0)),
            pl.BlockSpec((d, tn), lambda j: (0, j)),
        ],
        out_specs=pl.BlockSpec((nm, tn), lambda j: (0, j)),
        scratch_shapes=[pltpu.VMEM((nm, d), CDT)],
        compiler_params=pltpu.CompilerParams(
            dimension_semantics=("arbitrary",), vmem_limit_bytes=VMEM_LIMIT),
        name="memkv",
    )(mem2, g, w)


def _memattn_kernel(q_ref, z_ref, k_ref, v_ref, o_ref):
    s = _nt_dot(q_ref[...], k_ref[...]) * (MEM_HEAD_DIM ** -0.5)
    m = jnp.max(s, axis=1, keepdims=True)
    p = jnp.exp(s - m)
    l = jnp.sum(p, axis=1, keepdims=True)
    y = jnp.dot(p.astype(v_ref.dtype), v_ref[...], preferred_element_type=F32) / l
    o_ref[...] = (y * _silu(z_ref[...].astype(F32))).astype(o_ref.dtype)


def _memattn(p, mkv, tm):
    s = p.shape[0]
    nm = mkv.shape[0]
    hd = MEM_HEAD_DIM
    return pl.pallas_call(
        _memattn_kernel,
        out_shape=jax.ShapeDtypeStruct((s, WIDTH), CDT),
        grid=(s // tm, N_MEM_HEADS),
        in_specs=[
            pl.BlockSpec((tm, hd), lambda i, h: (i, C_QM // hd + h)),
            pl.BlockSpec((tm, hd), lambda i, h: (i, C_ZM // hd + h)),
            pl.BlockSpec((nm, hd), lambda i, h: (0, h)),
            pl.BlockSpec((nm, hd), lambda i, h: (0, N_MEM_HEADS + h)),
        ],
        out_specs=pl.BlockSpec((tm, hd), lambda i, h: (i, h)),
        compiler_params=pltpu.CompilerParams(
            dimension_semantics=("parallel", "parallel"), vmem_limit_bytes=VMEM_LIMIT),
        name="memattn",
    )(p, p, mkv, mkv)


def _merge_kernel(ua_ref, ub_ref, um_ref, ga_ref, gb_ref, gm_ref, b_ref, wa_ref, wb_ref, wm_ref, o_ref):
    d = o_ref.shape[1]
    acc = None
    for t, (u_ref, g_ref, w_ref) in enumerate(
            ((ua_ref, ga_ref, wa_ref), (ub_ref, gb_ref, wb_ref), (um_ref, gm_ref, wm_ref))):
        gate = jax.nn.sigmoid(g_ref[...].astype(F32) + b_ref[:, t * d:(t + 1) * d])
        term = gate * jnp.dot(u_ref[...], w_ref[...], preferred_element_type=F32)
        acc = term if acc is None else acc + term
    o_ref[...] = acc.astype(o_ref.dtype)


def _merge(ua, ub, um, p, b_gate, wa, wb, wm, tm):
    s = ua.shape[0]
    d = D_MODEL
    once = pl.Buffered(1)
    u_spec = pl.BlockSpec((tm, WIDTH), lambda i: (i, 0))
    w_spec = pl.BlockSpec((WIDTH, d), lambda i: (0, 0), pipeline_mode=once)
    return pl.pallas_call(
        _merge_kernel,
        out_shape=jax.ShapeDtypeStruct((s, d), CDT),
        grid=(s // tm,),
        in_specs=[
            u_spec, u_spec, u_spec,
            pl.BlockSpec((tm, d), lambda i: (i, C_GL // d + 0)),
            pl.BlockSpec((tm, d), lambda i: (i, C_GL // d + 1)),
            pl.BlockSpec((tm, d), lambda i: (i, C_GL // d + 2)),
            pl.BlockSpec((1, N_BRANCH * d), lambda i: (0, 0)),
            w_spec, w_spec, w_spec,
        ],
        out_specs=pl.BlockSpec((tm, d), lambda i: (i, 0)),
        compiler_params=pltpu.CompilerParams(
            dimension_semantics=("parallel",), vmem_limit_bytes=VMEM_LIMIT),
        name="merge",
    )(ua, ub, um, p, p, p, b_gate, wa, wb, wm)


def _out_kernel(x_ref, mg_ref, w_ref, g_ref, o_ref):
    y = x_ref[...] + jnp.dot(mg_ref[...], w_ref[...], preferred_element_type=F32)
    ms = jnp.mean(y * y, axis=-1, keepdims=True)
    o_ref[...] = (y * lax.rsqrt(ms + RMS_EPS) * g_ref[...]).astype(o_ref.dtype)


def _out(x2, merged, w_out, g_final, tm):
    s, d = x2.shape
    return pl.pallas_call(
        _out_kernel,
        out_shape=jax.ShapeDtypeStruct((s, d), x2.dtype),
        grid=(s // tm,),
        in_specs=[
            pl.BlockSpec((tm, d), lambda i: (i, 0)),
            pl.BlockSpec((tm, d), lambda i: (i, 0)),
            pl.BlockSpec((d, d), lambda i: (0, 0), pipeline_mode=pl.Buffered(1)),
            pl.BlockSpec((1, d), lambda i: (0, 0)),
        ],
        out_specs=pl.BlockSpec((tm, d), lambda i: (i, 0)),
        compiler_params=pltpu.CompilerParams(
            dimension_semantics=("parallel",), vmem_limit_bytes=VMEM_LIMIT),
        name="out",
    )(x2, merged, w_out, g_final)


def _reorder_w_in(w):
    o = 0
    parts = {}
    for name, width in (("qa", WIDTH), ("ka", WIDTH), ("va", WIDTH), ("za", WIDTH),
                        ("qi", N_IDX_HEADS * IDX_DIM), ("ki", IDX_DIM), ("wi", N_IDX_HEADS),
                        ("qb", WIDTH), ("kb", WIDTH), ("vb", WIDTH), ("zb", WIDTH),
                        ("qm", WIDTH), ("zm", WIDTH), ("gl", N_BRANCH * D_MODEL)):
        parts[name] = w[:, o:o + width]
        o += width
    assert o == w.shape[1]
    pad = jnp.zeros((w.shape[0], LANES - N_IDX_HEADS), w.dtype)
    out = jnp.concatenate(
        [parts["gl"], parts["qi"], parts["qa"], parts["ka"], parts["va"], parts["za"],
         parts["qb"], parts["kb"], parts["vb"], parts["zb"], parts["qm"], parts["zm"],
         parts["ki"], parts["ki"], parts["wi"], pad], axis=1)
    assert out.shape[1] == D_PAD
    return out.astype(CDT)


def kernel(x, mem, g_in, w_in, b_gate, lam_q1, lam_k1, lam_q2, lam_k2, g_subln, g_mem,
           w_mem_kv, w_br_a, w_br_b, w_br_m, w_out, g_final):
    bsz, s_len, d = x.shape
    assert bsz == 1 and d == D_MODEL and g_in.shape[0] == 1
    topk = min(TOPK_MAX, s_len // 4)
    tq = 256
    x2 = x.reshape(s_len, d)
    slopes = 2.0 ** (-8.0 * jnp.arange(1, N_HEADS_A + 1, dtype=F32) / N_HEADS_A)

    p = _proj(x2, g_in[0].reshape(1, d), _reorder_w_in(w_in[0]), tm=512, tn=768)
    ua = _dsa(p, slopes, tq, topk)
    lamv = jnp.stack([lam_q1[0], lam_k1[0], lam_q2[0], lam_k2[0]]).astype(F32)
    ub = _diff(p, slopes, lamv, g_subln[0].reshape(1, 2 * DIFF_DIM), tq)
    mkv = _memkv(mem.reshape(-1, d), g_mem[0].reshape(1, d), w_mem_kv[0], tn=512)
    um = _memattn(p, mkv, tm=1024)
    merged = _merge(ua, ub, um, p, b_gate[0].reshape(1, -1),
                    w_br_a[0].astype(CDT), w_br_b[0].astype(CDT), w_br_m[0].astype(CDT), tm=512)
    y = _out(x2, merged, w_out[0].astype(CDT), g_final.reshape(1, d), tm=512)
    return y.reshape(bsz, s_len, d)
```

```python
import functools

import jax
import jax.numpy as jnp
from jax import lax
from jax.experimental import pallas as pl
from jax.experimental.pallas import tpu as pltpu

F32 = jnp.float32
CDT = jnp.bfloat16

D_MODEL = 2048
N_HEADS_A = 8
HEAD_DIM_A = 128
N_IDX_HEADS = 16
IDX_DIM = 64
TOPK_MAX = 256
N_HEADS_B = 8
DIFF_DIM = 64
N_MEM_HEADS = 4
MEM_HEAD_DIM = 256
N_BRANCH = 3
WIDTH = 1024
RMS_EPS = 1e-6
NEG_INF = -1e30
F32_BIG = 3.0e38
LAM_INIT = 0.2
LANES = 128
LOG2E = 1.4426950408889634

C_GL = 0
C_QI = 6144
C_QA = 7168
C_KA = 8192
C_VA = 9216
C_ZA = 10240
C_QB = 11264
C_KB = 12288
C_VB = 13312
C_ZB = 14336
C_QM = 15360
C_ZM = 16384
C_KK = 17408
C_WI = 17536
D_PAD = 17664

VMEM_LIMIT = 56 * 1024 * 1024

BISECT_MAX_ITERS = 64


def _nt_dot(a, b):
    return lax.dot_general(a, b, (((1,), (1,)), ((), ())), preferred_element_type=F32)


def _silu(z):
    return z * jax.nn.sigmoid(z)


def _proj_kernel(x_ref, g_ref, w_ref, o_ref, h_ref):
    @pl.when(pl.program_id(1) == 0)
    def _():
        xf = x_ref[...]
        ms = jnp.mean(xf * xf, axis=-1, keepdims=True)
        h_ref[...] = (xf * lax.rsqrt(ms + RMS_EPS) * g_ref[...]).astype(h_ref.dtype)

    o_ref[...] = jnp.dot(h_ref[...], w_ref[...], preferred_element_type=F32).astype(o_ref.dtype)


def _proj(x2, g, w, tm, tn):
    s, d = x2.shape
    n = w.shape[1]
    return pl.pallas_call(
        _proj_kernel,
        out_shape=jax.ShapeDtypeStruct((s, n), CDT),
        grid=(s // tm, n // tn),
        in_specs=[
            pl.BlockSpec((tm, d), lambda i, j: (i, 0)),
            pl.BlockSpec((1, d), lambda i, j: (0, 0)),
            pl.BlockSpec((d, tn), lambda i, j: (0, j)),
        ],
        out_specs=pl.BlockSpec((tm, tn), lambda i, j: (i, j)),
        scratch_shapes=[pltpu.VMEM((tm, d), CDT)],
        compiler_params=pltpu.CompilerParams(
            dimension_semantics=("parallel", "arbitrary"), vmem_limit_bytes=VMEM_LIMIT),
        name="proj",
    )(x2, g, w)


def _tile_lanes(x, n):
    return x if n == 1 else jnp.concatenate([x] * n, axis=1)


def _flash_scratch(rows, tk):
    return [pltpu.VMEM((rows, tk), F32), pltpu.VMEM((rows, tk), F32),
            pltpu.VMEM((rows, tk), CDT), pltpu.VMEM((rows, tk), CDT),
            pltpu.VMEM((rows, LANES), F32), pltpu.VMEM((rows, LANES), F32),
            pltpu.VMEM((rows, LANES), F32), pltpu.VMEM((rows, 2 * LANES), F32)]


def _flash_chunks(n_full, qk, v1, shift, last_valid, s_a, s_b, p_a, p_b, al_a, al_b, m_ref, acc_ref):
    tk = s_a.shape[1]

    def soft(s_ref, p_ref, al_ref, c, valid=None):
        s = s_ref[...]
        if valid is not None:
            s = jnp.where(valid, s, NEG_INF)
        d = shift(c)
        m = m_ref[...]
        m_new = jnp.maximum(m, jnp.max(s, axis=1, keepdims=True) - d)
        al_ref[...] = jnp.exp2(m - m_new)
        m_ref[...] = m_new
        p_ref[...] = jnp.exp2(s - _tile_lanes(m_new + d, tk // LANES)).astype(p_ref.dtype)

    def pv(p_ref, al_ref, c):
        acc_ref[...] = (_tile_lanes(al_ref[...], 2) * acc_ref[...]
                        + jnp.dot(p_ref[...], v1(c), preferred_element_type=F32))

    m_ref[...] = jnp.full(m_ref.shape, NEG_INF, F32)
    acc_ref[...] = jnp.zeros(acc_ref.shape, F32)
    s_a[...] = qk(0)
    p_b[...] = jnp.zeros(p_b.shape, p_b.dtype)
    al_b[...] = jnp.ones(al_b.shape, F32)

    def body(j, _):
        c0 = 2 * j
        pv(p_b, al_b, jnp.maximum(c0 - 1, 0))
        s_b[...] = qk(c0 + 1)
        soft(s_a, p_a, al_a, c0)
        pv(p_a, al_a, c0)
        s_a[...] = qk(c0 + 2)
        soft(s_b, p_b, al_b, c0 + 1)
        return 0

    lax.fori_loop(0, n_full // 2, body, 0)

    @pl.when(n_full % 2 == 1)
    def _():
        c0 = n_full - 1
        pv(p_b, al_b, jnp.maximum(c0 - 1, 0))
        s_b[...] = qk(n_full)
        soft(s_a, p_a, al_a, c0)
        pv(p_a, al_a, c0)
        soft(s_b, p_b, al_b, n_full, last_valid)
        pv(p_b, al_b, n_full)

    @pl.when(n_full % 2 == 0)
    def _():
        pv(p_b, al_b, jnp.maximum(n_full - 1, 0))
        soft(s_a, p_a, al_a, n_full, last_valid)
        pv(p_a, al_a, n_full)


def _dsa_kernel(slopes_ref, qa_ref, ka_ref, va_ref, za_ref, qi_ref, kk_ref, wi_ref, o_ref,
                sc_ref, qz_ref, wb_ref, *scratch, tq, tk, topk):
    qi = pl.program_id(0)
    h = pl.program_id(1)
    q0 = qi * tq
    n_full = (q0 + 1) // tk
    row = lax.broadcasted_iota(jnp.int32, (tq, tk), 0)
    col = lax.broadcasted_iota(jnp.int32, (tq, tk), 1)
    rc = row - col
    nsub = tk // LANES
    rblk = 128

    def keys(kc):
        return pl.ds(pl.multiple_of(kc * tk, tk), tk)

    def fold(a, op):
        r = a[:, :LANES]
        for t in range(1, nsub):
            r = op(r, a[:, t * LANES:(t + 1) * LANES])
        return r

    @pl.when(h == 0)
    def _select():
        lane = lax.broadcasted_iota(jnp.int32, (tq, LANES), 1)
        for m in range(N_IDX_HEADS // 2):
            qp = qi_ref[:, m * LANES:(m + 1) * LANES].astype(F32)
            qz_ref[2 * m] = jnp.where(lane < IDX_DIM, qp, 0.0).astype(qz_ref.dtype)
            qz_ref[2 * m + 1] = jnp.where(lane >= IDX_DIM, qp, 0.0).astype(qz_ref.dtype)
        wi = wi_ref[...].astype(F32) * (IDX_DIM ** -0.5 * N_IDX_HEADS ** -0.5)
        for j in range(N_IDX_HEADS):
            wb_ref[j] = jnp.broadcast_to(wi[:, j:j + 1], (tq, LANES))

        def chunk_scores(kc):
            kk = kk_ref[keys(kc), :]
            acc = jnp.zeros((tq, tk), F32)
            for j in range(N_IDX_HEADS):
                acc = acc + jnp.maximum(_nt_dot(qz_ref[j], kk), 0.0) * _tile_lanes(wb_ref[j], nsub)
            return acc

        def full_body(kc, carry):
            rmin, rmax = carry
            a = chunk_scores(kc)
            sc_ref[kc] = a
            return jnp.minimum(rmin, fold(a, jnp.minimum)), jnp.maximum(rmax, fold(a, jnp.maximum))

        rmin, rmax = lax.fori_loop(
            0, n_full, full_body,
            (jnp.full((tq, LANES), F32_BIG, F32), jnp.full((tq, LANES), -F32_BIG, F32)))
        a = chunk_scores(n_full)
        causal = rc + (q0 - n_full * tk) >= 0
        sc_ref[n_full] = jnp.where(causal, a, -F32_BIG)
        rmin = jnp.minimum(rmin, fold(jnp.where(causal, a, F32_BIG), jnp.minimum))
        rmax = jnp.maximum(rmax, fold(jnp.where(causal, a, -F32_BIG), jnp.maximum))
        lo0 = jnp.min(rmin, axis=1, keepdims=True)
        hi0 = jnp.max(rmax, axis=1, keepdims=True)

        def count_ge(t):
            tb = jnp.broadcast_to(t, (tq, LANES))
            out = []
            for r in range(tq // rblk):
                rows = slice(r * rblk, (r + 1) * rblk)
                tt = _tile_lanes(tb[rows], nsub)

                def cbody(kc, cnt, rows=rows, tt=tt):
                    return cnt + fold(jnp.where(sc_ref[kc, rows, :] >= tt, 1.0, 0.0), jnp.add)

                cnt = lax.fori_loop(0, n_full + 1, cbody, jnp.zeros((rblk, LANES), F32))
                out.append(jnp.sum(cnt, axis=1, keepdims=True))
            return jnp.concatenate(out, axis=0)

        kf = float(topk)
        n_causal = (q0 + row[:, :1] + 1).astype(F32)

        def cond(c):
            return jnp.logical_and(c[0] < BISECT_MAX_ITERS, c[4])

        def body(c):
            it, lo, hi, cnt_lo, _ = c
            mid = lo + 0.5 * (hi - lo)
            cm = count_ge(mid)
            ge = cm >= kf
            moving = jnp.logical_and(mid > lo, mid < hi)
            lo = jnp.where(ge, mid, lo)
            hi = jnp.where(ge, hi, mid)
            cnt_lo = jnp.where(ge, cm, cnt_lo)
            active = jnp.logical_and(cnt_lo > kf, moving)
            flag = jnp.max(jnp.where(active, 1.0, 0.0)) > 0.0
            return it + 1, lo, hi, cnt_lo, flag

        c0 = (jnp.int32(0), lo0, hi0, n_causal, jnp.max(n_causal) > kf)
        _, lo, _, _, _ = lax.while_loop(cond, body, c0)
        lob = _tile_lanes(jnp.broadcast_to(lo, (tq, LANES)), nsub)

        def mbody(kc, _):
            sc_ref[kc] = jnp.where(sc_ref[kc] >= lob, 0.0, NEG_INF)
            return 0

        lax.fori_loop(0, n_full + 1, mbody, 0)

    cs = slopes_ref[h] * LOG2E
    q = (qa_ref[...].astype(F32) * (HEAD_DIM_A ** -0.5 * LOG2E)).astype(qa_ref.dtype)
    base = -cs * rc.astype(F32)
    ones = jnp.ones((tk, LANES), va_ref.dtype)

    def qk(kc):
        return _nt_dot(q, ka_ref[keys(kc), :]) + base + sc_ref[kc]

    def v1(kc):
        return jnp.concatenate([va_ref[keys(kc), :], ones], axis=1)

    def shift(kc):
        return cs * (q0 - kc * tk).astype(F32)

    _flash_chunks(n_full, qk, v1, shift, None, *scratch)
    acc = scratch[-1][...]
    y = acc[:, :LANES] / acc[:, LANES:]
    o_ref[...] = (y * _silu(za_ref[...].astype(F32))).astype(o_ref.dtype)


def _dsa(p, slopes, tq, tk, topk):
    s = p.shape[0]
    nq = s // tq
    assert tk >= tq and s % tk == 0
    cb = lambda c: c // LANES
    kern = functools.partial(_dsa_kernel, tq=tq, tk=tk, topk=topk)
    return pl.pallas_call(
        kern,
        out_shape=jax.ShapeDtypeStruct((s, WIDTH), CDT),
        grid=(nq, N_HEADS_A),
        in_specs=[
            pl.BlockSpec(memory_space=pltpu.SMEM),
            pl.BlockSpec((tq, LANES), lambda i, h: (i, cb(C_QA) + h)),
            pl.BlockSpec((s, LANES), lambda i, h: (0, cb(C_KA) + h)),
            pl.BlockSpec((s, LANES), lambda i, h: (0, cb(C_VA) + h)),
            pl.BlockSpec((tq, LANES), lambda i, h: (i, cb(C_ZA) + h)),
            pl.BlockSpec((tq, WIDTH), lambda i, h: (i, C_QI // WIDTH)),
            pl.BlockSpec((s, LANES), lambda i, h: (0, cb(C_KK))),
            pl.BlockSpec((tq, LANES), lambda i, h: (i, cb(C_WI))),
        ],
        out_specs=pl.BlockSpec((tq, LANES), lambda i, h: (i, h)),
        scratch_shapes=[
            pltpu.VMEM((s // tk, tq, tk), F32),
            pltpu.VMEM((N_IDX_HEADS, tq, LANES), CDT),
            pltpu.VMEM((N_IDX_HEADS, tq, LANES), F32),
        ] + _flash_scratch(tq, tk),
        compiler_params=pltpu.CompilerParams(
            dimension_semantics=("parallel", "arbitrary"), vmem_limit_bytes=VMEM_LIMIT),
        name="dsa",
    )(slopes, p, p, p, p, p, p, p)


def _diff_kernel(slopes_ref, lam_ref, qb_ref, kb_ref, vb_ref, zb_ref, g_ref, o_ref, *scratch, tq, tk):
    h = pl.program_id(0)
    qi = pl.program_id(1)
    q0 = qi * tq
    cs = slopes_ref[h] * LOG2E
    lane = lax.broadcasted_iota(jnp.int32, (tq, LANES), 1)
    qf = qb_ref[...].astype(F32) * (DIFF_DIM ** -0.5 * LOG2E)
    qs = jnp.concatenate([jnp.where(lane < DIFF_DIM, qf, 0.0),
                          jnp.where(lane >= DIFF_DIM, qf, 0.0)], axis=0).astype(qb_ref.dtype)
    row = lax.broadcasted_iota(jnp.int32, (2 * tq, tk), 0)
    row = jnp.where(row >= tq, row - tq, row)
    col = lax.broadcasted_iota(jnp.int32, (2 * tq, tk), 1)
    rc = row - col
    base = -cs * rc.astype(F32)
    ones = jnp.ones((tk, LANES), vb_ref.dtype)

    def keys(kc):
        return pl.ds(pl.multiple_of(kc * tk, tk), tk)

    def qk(kc):
        return _nt_dot(qs, kb_ref[keys(kc), :]) + base

    def v1(kc):
        return jnp.concatenate([vb_ref[keys(kc), :], ones], axis=1)

    def shift(kc):
        return cs * (q0 - kc * tk).astype(F32)

    n_full = (q0 + 1) // tk
    valid = rc + (q0 - n_full * tk) >= 0
    _flash_chunks(n_full, qk, v1, shift, valid, *scratch)
    acc = scratch[-1][...]

    lv = lam_ref[...]
    lam = (jnp.exp(jnp.sum(lv[0:1] * lv[1:2], axis=1, keepdims=True))
           - jnp.exp(jnp.sum(lv[2:3] * lv[3:4], axis=1, keepdims=True)) + LAM_INIT)
    y = (acc[:tq, :LANES] / acc[:tq, LANES:]) - lam * (acc[tq:, :LANES] / acc[tq:, LANES:])
    ms = jnp.mean(y * y, axis=-1, keepdims=True)
    y = (y * lax.rsqrt(ms + RMS_EPS) * g_ref[...]) * (1.0 - LAM_INIT)
    o_ref[...] = (y * _silu(zb_ref[...].astype(F32))).astype(o_ref.dtype)


def _diff(p, slopes, lamv, g_subln, tq, tk):
    s = p.shape[0]
    nq = s // tq
    assert tk >= tq and s % tk == 0
    cb = lambda c: c // LANES
    kern = functools.partial(_diff_kernel, tq=tq, tk=tk)
    return pl.pallas_call(
        kern,
        out_shape=jax.ShapeDtypeStruct((s, WIDTH), CDT),
        grid=(N_HEADS_B, nq),
        in_specs=[
            pl.BlockSpec(memory_space=pltpu.SMEM),
            pl.BlockSpec((4, DIFF_DIM), lambda h, i: (0, 0)),
            pl.BlockSpec((tq, LANES), lambda h, i: (i, cb(C_QB) + h)),
            pl.BlockSpec((s, LANES), lambda h, i: (0, cb(C_KB) + h)),
            pl.BlockSpec((s, LANES), lambda h, i: (0, cb(C_VB) + h)),
            pl.BlockSpec((tq, LANES), lambda h, i: (i, cb(C_ZB) + h)),
            pl.BlockSpec((1, LANES), lambda h, i: (0, 0)),
        ],
        out_specs=pl.BlockSpec((tq, LANES), lambda h, i: (i, h)),
        scratch_shapes=_flash_scratch(2 * tq, tk),
        compiler_params=pltpu.CompilerParams(
            dimension_semantics=("parallel", "parallel"), vmem_limit_bytes=VMEM_LIMIT),
        name="diff",
    )(slopes, lamv, p, p, p, p, g_subln)


def _memkv_kernel(mem_ref, g_ref, w_ref, o_ref, h_ref):
    @pl.when(pl.program_id(0) == 0)
    def _():
        xf = mem_ref[...]
        ms = jnp.mean(xf * xf, axis=-1, keepdims=True)
        h_ref[...] = (xf * lax.rsqrt(ms + RMS_EPS) * g_ref[...]).astype(h_ref.dtype)

    o_ref[...] = jnp.dot(h_ref[...], w_ref[...].astype(h_ref.dtype),
                         preferred_element_type=F32).astype(o_ref.dtype)


def _memkv(mem2, g, w, tn):
    nm, d = mem2.shape
    n = w.shape[1]
    return pl.pallas_call(
        _memkv_kernel,
        out_shape=jax.ShapeDtypeStruct((nm, n), CDT),
        grid=(n // tn,),
        in_specs=[
            pl.BlockSpec((nm, d), lambda j: (0, 0)),
            pl.BlockSpec((1, d), lambda j: (0, 0)),
            pl.BlockSpec((d, tn), lambda j: (0, j)),
        ],
        out_specs=pl.BlockSpec((nm, tn), lambda j: (0, j)),
        scratch_shapes=[pltpu.VMEM((nm, d), CDT)],
        compiler_params=pltpu.CompilerParams(
            dimension_semantics=("arbitrary",), vmem_limit_bytes=VMEM_LIMIT),
        name="memkv",
    )(mem2, g, w)


def _memattn_kernel(q_ref, z_ref, k_ref, v_ref, o_ref):
    s = _nt_dot(q_ref[...], k_ref[...]) * (MEM_HEAD_DIM ** -0.5)
    m = jnp.max(s, axis=1, keepdims=True)
    p = jnp.exp(s - m)
    l = jnp.sum(p, axis=1, keepdims=True)
    y = jnp.dot(p.astype(v_ref.dtype), v_ref[...], preferred_element_type=F32) / l
    o_ref[...] = (y * _silu(z_ref[...].astype(F32))).astype(o_ref.dtype)


def _memattn(p, mkv, tm):
    s = p.shape[0]
    nm = mkv.shape[0]
    hd = MEM_HEAD_DIM
    return pl.pallas_call(
        _memattn_kernel,
        out_shape=jax.ShapeDtypeStruct((s, WIDTH), CDT),
        grid=(s // tm, N_MEM_HEADS),
        in_specs=[
            pl.BlockSpec((tm, hd), lambda i, h: (i, C_QM // hd + h)),
            pl.BlockSpec((tm, hd), lambda i, h: (i, C_ZM // hd + h)),
            pl.BlockSpec((nm, hd), lambda i, h: (0, h)),
            pl.BlockSpec((nm, hd), lambda i, h: (0, N_MEM_HEADS + h)),
        ],
        out_specs=pl.BlockSpec((tm, hd), lambda i, h: (i, h)),
        compiler_params=pltpu.CompilerParams(
            dimension_semantics=("parallel", "parallel"), vmem_limit_bytes=VMEM_LIMIT),
        name="memattn",
    )(p, p, mkv, mkv)


def _merge_kernel(ua_ref, ub_ref, um_ref, ga_ref, gb_ref, gm_ref, b_ref, wa_ref, wb_ref, wm_ref, o_ref):
    d = o_ref.shape[1]
    acc = None
    for t, (u_ref, g_ref, w_ref) in enumerate(
            ((ua_ref, ga_ref, wa_ref), (ub_ref, gb_ref, wb_ref), (um_ref, gm_ref, wm_ref))):
        gate = jax.nn.sigmoid(g_ref[...].astype(F32) + b_ref[:, t * d:(t + 1) * d])
        term = gate * jnp.dot(u_ref[...], w_ref[...], preferred_element_type=F32)
        acc = term if acc is None else acc + term
    o_ref[...] = acc.astype(o_ref.dtype)


def _merge(ua, ub, um, p, b_gate, wa, wb, wm, tm):
    s = ua.shape[0]
    d = D_MODEL
    once = pl.Buffered(1)
    u_spec = pl.BlockSpec((tm, WIDTH), lambda i: (i, 0))
    w_spec = pl.BlockSpec((WIDTH, d), lambda i: (0, 0), pipeline_mode=once)
    return pl.pallas_call(
        _merge_kernel,
        out_shape=jax.ShapeDtypeStruct((s, d), CDT),
        grid=(s // tm,),
        in_specs=[
            u_spec, u_spec, u_spec,
            pl.BlockSpec((tm, d), lambda i: (i, C_GL // d + 0)),
            pl.BlockSpec((tm, d), lambda i: (i, C_GL // d + 1)),
            pl.BlockSpec((tm, d), lambda i: (i, C_GL // d + 2)),
            pl.BlockSpec((1, N_BRANCH * d), lambda i: (0, 0)),
            w_spec, w_spec, w_spec,
        ],
        out_specs=pl.BlockSpec((tm, d), lambda i: (i, 0)),
        compiler_params=pltpu.CompilerParams(
            dimension_semantics=("parallel",), vmem_limit_bytes=VMEM_LIMIT),
        name="merge",
    )(ua, ub, um, p, p, p, b_gate, wa, wb, wm)


def _out_kernel(x_ref, mg_ref, w_ref, g_ref, o_ref):
    y = x_ref[...] + jnp.dot(mg_ref[...], w_ref[...], preferred_element_type=F32)
    ms = jnp.mean(y * y, axis=-1, keepdims=True)
    o_ref[...] = (y * lax.rsqrt(ms + RMS_EPS) * g_ref[...]).astype(o_ref.dtype)


def _out(x2, merged, w_out, g_final, tm):
    s, d = x2.shape
    return pl.pallas_call(
        _out_kernel,
        out_shape=jax.ShapeDtypeStruct((s, d), x2.dtype),
        grid=(s // tm,),
        in_specs=[
            pl.BlockSpec((tm, d), lambda i: (i, 0)),
            pl.BlockSpec((tm, d), lambda i: (i, 0)),
            pl.BlockSpec((d, d), lambda i: (0, 0), pipeline_mode=pl.Buffered(1)),
            pl.BlockSpec((1, d), lambda i: (0, 0)),
        ],
        out_specs=pl.BlockSpec((tm, d), lambda i: (i, 0)),
        compiler_params=pltpu.CompilerParams(
            dimension_semantics=("parallel",), vmem_limit_bytes=VMEM_LIMIT),
        name="out",
    )(x2, merged, w_out, g_final)


def _reorder_w_in(w):
    o = 0
    parts = {}
    for name, width in (("qa", WIDTH), ("ka", WIDTH), ("va", WIDTH), ("za", WIDTH),
                        ("qi", N_IDX_HEADS * IDX_DIM), ("ki", IDX_DIM), ("wi", N_IDX_HEADS),
                        ("qb", WIDTH), ("kb", WIDTH), ("vb", WIDTH), ("zb", WIDTH),
                        ("qm", WIDTH), ("zm", WIDTH), ("gl", N_BRANCH * D_MODEL)):
        parts[name] = w[:, o:o + width]
        o += width
    assert o == w.shape[1]
    pad = jnp.zeros((w.shape[0], LANES - N_IDX_HEADS), w.dtype)
    out = jnp.concatenate(
        [parts["gl"], parts["qi"], parts["qa"], parts["ka"], parts["va"], parts["za"],
         parts["qb"], parts["kb"], parts["vb"], parts["zb"], parts["qm"], parts["zm"],
         parts["ki"], parts["ki"], parts["wi"], pad], axis=1)
    assert out.shape[1] == D_PAD
    return out.astype(CDT)


def kernel(x, mem, g_in, w_in, b_gate, lam_q1, lam_k1, lam_q2, lam_k2, g_subln, g_mem,
           w_mem_kv, w_br_a, w_br_b, w_br_m, w_out, g_final):
    bsz, s_len, d = x.shape
    assert bsz == 1 and d == D_MODEL and g_in.shape[0] == 1
    topk = min(TOPK_MAX, s_len // 4)
    x2 = x.reshape(s_len, d)
    slopes = 2.0 ** (-8.0 * jnp.arange(1, N_HEADS_A + 1, dtype=F32) / N_HEADS_A)

    p = _proj(x2, g_in[0].reshape(1, d), _reorder_w_in(w_in[0]), tm=512, tn=768)
    ua = _dsa(p, slopes, tq=512, tk=512, topk=topk)
    lamv = jnp.stack([lam_q1[0], lam_k1[0], lam_q2[0], lam_k2[0]]).astype(F32)
    ub = _diff(p, slopes, lamv, g_subln[0].reshape(1, 2 * DIFF_DIM), tq=256, tk=512)
    mkv = _memkv(mem.reshape(-1, d), g_mem[0].reshape(1, d), w_mem_kv[0], tn=512)
    um = _memattn(p, mkv, tm=1024)
    merged = _merge(ua, ub, um, p, b_gate[0].reshape(1, -1),
                    w_br_a[0].astype(CDT), w_br_b[0].astype(CDT), w_br_m[0].astype(CDT), tm=512)
    y = _out(x2, merged, w_out[0].astype(CDT), g_final.reshape(1, d), tm=512)
    return y.reshape(bsz, s_len, d)
```

```python
import functools

import jax
import jax.numpy as jnp
from jax import lax
from jax.experimental import pallas as pl
from jax.experimental.pallas import tpu as pltpu

F32 = jnp.float32
CDT = jnp.bfloat16

D_MODEL = 2048
N_HEADS_A = 8
HEAD_DIM_A = 128
N_IDX_HEADS = 16
IDX_DIM = 64
TOPK_MAX = 256
N_HEADS_B = 8
DIFF_DIM = 64
N_MEM_HEADS = 4
MEM_HEAD_DIM = 256
N_BRANCH = 3
WIDTH = 1024
RMS_EPS = 1e-6
NEG_INF = -1e30
F32_BIG = 3.0e38
LAM_INIT = 0.2
LANES = 128
LOG2E = 1.4426950408889634

C_GL = 0
C_QI = 6144
C_QA = 7168
C_KA = 8192
C_VA = 9216
C_ZA = 10240
C_QB = 11264
C_KB = 12288
C_VB = 13312
C_ZB = 14336
C_QM = 15360
C_ZM = 16384
C_KK = 17408
C_WI = 17536
D_PAD = 17664

VMEM_LIMIT = 56 * 1024 * 1024

BISECT_MAX_ITERS = 64


def _nt_dot(a, b):
    return lax.dot_general(a, b, (((1,), (1,)), ((), ())), preferred_element_type=F32)


def _silu(z):
    return z * jax.nn.sigmoid(z)


def _proj_kernel(x_ref, g_ref, w_ref, o_ref, h_ref):
    @pl.when(pl.program_id(1) == 0)
    def _():
        xf = x_ref[...]
        ms = jnp.mean(xf * xf, axis=-1, keepdims=True)
        h_ref[...] = (xf * lax.rsqrt(ms + RMS_EPS) * g_ref[...]).astype(h_ref.dtype)

    o_ref[...] = _nt_dot(h_ref[...], w_ref[...]).astype(o_ref.dtype)


def _proj(x2, g, wt, tm, tn):
    s, d = x2.shape
    n = wt.shape[0]
    return pl.pallas_call(
        _proj_kernel,
        out_shape=jax.ShapeDtypeStruct((s, n), CDT),
        grid=(s // tm, n // tn),
        in_specs=[
            pl.BlockSpec((tm, d), lambda i, j: (i, 0)),
            pl.BlockSpec((1, d), lambda i, j: (0, 0)),
            pl.BlockSpec((tn, d), lambda i, j: (j, 0)),
        ],
        out_specs=pl.BlockSpec((tm, tn), lambda i, j: (i, j)),
        scratch_shapes=[pltpu.VMEM((tm, d), CDT)],
        compiler_params=pltpu.CompilerParams(
            dimension_semantics=("parallel", "arbitrary"), vmem_limit_bytes=VMEM_LIMIT),
        name="proj",
    )(x2, g, wt)


def _tile_lanes(x, n):
    return x if n == 1 else jnp.concatenate([x] * n, axis=1)


def _flash_scratch(rows, tk):
    return [pltpu.VMEM((rows, tk), F32), pltpu.VMEM((rows, tk), F32),
            pltpu.VMEM((rows, tk), CDT), pltpu.VMEM((rows, tk), CDT),
            pltpu.VMEM((rows, LANES), F32), pltpu.VMEM((rows, LANES), F32),
            pltpu.VMEM((rows, LANES), F32), pltpu.VMEM((rows, 2 * LANES), F32)]


def _flash_chunks(n_full, qk, v1, shift, last_valid, s_a, s_b, p_a, p_b, al_a, al_b, m_ref, acc_ref):
    tk = s_a.shape[1]

    def soft(s_ref, p_ref, al_ref, c, valid=None):
        s = s_ref[...]
        if valid is not None:
            s = jnp.where(valid, s, NEG_INF)
        d = shift(c)
        m = m_ref[...]
        m_new = jnp.maximum(m, jnp.max(s, axis=1, keepdims=True) - d)
        al_ref[...] = jnp.exp2(m - m_new)
        m_ref[...] = m_new
        p_ref[...] = jnp.exp2(s - _tile_lanes(m_new + d, tk // LANES)).astype(p_ref.dtype)

    def pv(p_ref, al_ref, c):
        acc_ref[...] = (_tile_lanes(al_ref[...], 2) * acc_ref[...]
                        + jnp.dot(p_ref[...], v1(c), preferred_element_type=F32))

    m_ref[...] = jnp.full(m_ref.shape, NEG_INF, F32)
    acc_ref[...] = jnp.zeros(acc_ref.shape, F32)
    s_a[...] = qk(0)
    p_b[...] = jnp.zeros(p_b.shape, p_b.dtype)
    al_b[...] = jnp.ones(al_b.shape, F32)

    def body(j, _):
        c0 = 2 * j
        pv(p_b, al_b, jnp.maximum(c0 - 1, 0))
        s_b[...] = qk(c0 + 1)
        soft(s_a, p_a, al_a, c0)
        pv(p_a, al_a, c0)
        s_a[...] = qk(c0 + 2)
        soft(s_b, p_b, al_b, c0 + 1)
        return 0

    lax.fori_loop(0, n_full // 2, body, 0)

    @pl.when(n_full % 2 == 1)
    def _():
        c0 = n_full - 1
        pv(p_b, al_b, jnp.maximum(c0 - 1, 0))
        s_b[...] = qk(n_full)
        soft(s_a, p_a, al_a, c0)
        pv(p_a, al_a, c0)
        soft(s_b, p_b, al_b, n_full, last_valid)
        pv(p_b, al_b, n_full)

    @pl.when(n_full % 2 == 0)
    def _():
        pv(p_b, al_b, jnp.maximum(n_full - 1, 0))
        soft(s_a, p_a, al_a, n_full, last_valid)
        pv(p_a, al_a, n_full)


def _dsa_kernel(slopes_ref, qa_ref, ka_ref, va_ref, za_ref, qi_ref, kk_ref, wi_ref, o_ref,
                sc_ref, qz_ref, wb_ref, *scratch, tq, tk, topk):
    qi = pl.program_id(0)
    h = pl.program_id(1)
    q0 = qi * tq
    n_full = (q0 + 1) // tk
    row = lax.broadcasted_iota(jnp.int32, (tq, tk), 0)
    col = lax.broadcasted_iota(jnp.int32, (tq, tk), 1)
    rc = row - col
    nsub = tk // LANES
    rblk = 128

    def keys(kc):
        return pl.ds(pl.multiple_of(kc * tk, tk), tk)

    def fold(a, op):
        r = a[:, :LANES]
        for t in range(1, nsub):
            r = op(r, a[:, t * LANES:(t + 1) * LANES])
        return r

    @pl.when(h == 0)
    def _select():
        lane = lax.broadcasted_iota(jnp.int32, (tq, LANES), 1)
        for m in range(N_IDX_HEADS // 2):
            qp = qi_ref[:, m * LANES:(m + 1) * LANES].astype(F32)
            qz_ref[2 * m] = jnp.where(lane < IDX_DIM, qp, 0.0).astype(qz_ref.dtype)
            qz_ref[2 * m + 1] = jnp.where(lane >= IDX_DIM, qp, 0.0).astype(qz_ref.dtype)
        wi = wi_ref[...].astype(F32) * (IDX_DIM ** -0.5 * N_IDX_HEADS ** -0.5)
        for j in range(N_IDX_HEADS):
            wb_ref[j] = jnp.broadcast_to(wi[:, j:j + 1], (tq, LANES))

        def chunk_scores(kc):
            kk = kk_ref[keys(kc), :]
            acc = jnp.zeros((tq, tk), F32)
            for j in range(N_IDX_HEADS):
                acc = acc + jnp.maximum(_nt_dot(qz_ref[j], kk), 0.0) * _tile_lanes(wb_ref[j], nsub)
            return acc

        def full_body(kc, carry):
            rmin, rmax = carry
            a = chunk_scores(kc)
            sc_ref[kc] = a
            return jnp.minimum(rmin, fold(a, jnp.minimum)), jnp.maximum(rmax, fold(a, jnp.maximum))

        rmin, rmax = lax.fori_loop(
            0, n_full, full_body,
            (jnp.full((tq, LANES), F32_BIG, F32), jnp.full((tq, LANES), -F32_BIG, F32)))
        a = chunk_scores(n_full)
        causal = rc + (q0 - n_full * tk) >= 0
        sc_ref[n_full] = jnp.where(causal, a, -F32_BIG)
        rmin = jnp.minimum(rmin, fold(jnp.where(causal, a, F32_BIG), jnp.minimum))
        rmax = jnp.maximum(rmax, fold(jnp.where(causal, a, -F32_BIG), jnp.maximum))
        lo0 = jnp.min(rmin, axis=1, keepdims=True)
        hi0 = jnp.max(rmax, axis=1, keepdims=True)

        sc_ref[n_full + 1] = jnp.full((tq, tk), -F32_BIG, F32)
        lane_ones = jnp.ones((LANES, LANES), CDT)

        def count_ge(tb):
            cnts = []
            for r in range(tq // rblk):
                rows = slice(r * rblk, (r + 1) * rblk)
                tt = _tile_lanes(tb[rows], nsub)

                def cbody(j, cnt, rows=rows, tt=tt):
                    for u in range(2):
                        ind = jnp.where(sc_ref[2 * j + u, rows, :] >= tt, 1.0, 0.0)
                        cnt = cnt + fold(ind, jnp.add)
                    return cnt

                cnts.append(lax.fori_loop(0, (n_full + 2) // 2, cbody, jnp.zeros((rblk, LANES), F32)))
            cnt = jnp.concatenate(cnts, axis=0).astype(CDT)
            return jnp.dot(cnt, lane_ones, preferred_element_type=F32)

        kf = float(topk)
        n_causal = (q0 + row[:, :LANES] + 1).astype(F32)
        lo0 = jnp.broadcast_to(lo0, (tq, LANES))
        hi0 = jnp.broadcast_to(hi0, (tq, LANES))

        def cond(c):
            return jnp.logical_and(c[0] < BISECT_MAX_ITERS, c[4])

        def body(c):
            it, lo, hi, cnt_lo, _ = c
            mid = lo + 0.5 * (hi - lo)
            cm = count_ge(mid)
            ge = cm >= kf
            moving = jnp.logical_and(mid > lo, mid < hi)
            lo = jnp.where(ge, mid, lo)
            hi = jnp.where(ge, hi, mid)
            cnt_lo = jnp.where(ge, cm, cnt_lo)
            active = jnp.logical_and(cnt_lo > kf, moving)
            flag = jnp.max(jnp.where(active, 1.0, 0.0)) > 0.0
            return it + 1, lo, hi, cnt_lo, flag

        c0 = (jnp.int32(0), lo0, hi0, n_causal, jnp.max(n_causal) > kf)
        _, lo, _, _, _ = lax.while_loop(cond, body, c0)
        lob = _tile_lanes(lo, nsub)

        def mbody(kc, _):
            sc_ref[kc] = jnp.where(sc_ref[kc] >= lob, 0.0, NEG_INF)
            return 0

        lax.fori_loop(0, n_full + 1, mbody, 0)

    cs = slopes_ref[h] * LOG2E
    q = (qa_ref[...].astype(F32) * (HEAD_DIM_A ** -0.5 * LOG2E)).astype(qa_ref.dtype)
    base = -cs * rc.astype(F32)
    ones = jnp.ones((tk, LANES), va_ref.dtype)

    def qk(kc):
        return _nt_dot(q, ka_ref[keys(kc), :]) + base + sc_ref[kc]

    def v1(kc):
        return jnp.concatenate([va_ref[keys(kc), :], ones], axis=1)

    def shift(kc):
        return cs * (q0 - kc * tk).astype(F32)

    _flash_chunks(n_full, qk, v1, shift, None, *scratch)
    acc = scratch[-1][...]
    y = acc[:, :LANES] / acc[:, LANES:]
    o_ref[...] = (y * _silu(za_ref[...].astype(F32))).astype(o_ref.dtype)


def _dsa(p, slopes, tq, tk, topk):
    s = p.shape[0]
    nq = s // tq
    assert tk >= tq and s % tk == 0
    cb = lambda c: c // LANES
    kern = functools.partial(_dsa_kernel, tq=tq, tk=tk, topk=topk)
    return pl.pallas_call(
        kern,
        out_shape=jax.ShapeDtypeStruct((s, WIDTH), CDT),
        grid=(nq, N_HEADS_A),
        in_specs=[
            pl.BlockSpec(memory_space=pltpu.SMEM),
            pl.BlockSpec((tq, LANES), lambda i, h: (i, cb(C_QA) + h)),
            pl.BlockSpec((s, LANES), lambda i, h: (0, cb(C_KA) + h)),
            pl.BlockSpec((s, LANES), lambda i, h: (0, cb(C_VA) + h)),
            pl.BlockSpec((tq, LANES), lambda i, h: (i, cb(C_ZA) + h)),
            pl.BlockSpec((tq, WIDTH), lambda i, h: (i, C_QI // WIDTH)),
            pl.BlockSpec((s, LANES), lambda i, h: (0, cb(C_KK))),
            pl.BlockSpec((tq, LANES), lambda i, h: (i, cb(C_WI))),
        ],
        out_specs=pl.BlockSpec((tq, LANES), lambda i, h: (i, h)),
        scratch_shapes=[
            pltpu.VMEM((s // tk + 1, tq, tk), F32),
            pltpu.VMEM((N_IDX_HEADS, tq, LANES), CDT),
            pltpu.VMEM((N_IDX_HEADS, tq, LANES), F32),
        ] + _flash_scratch(tq, tk),
        compiler_params=pltpu.CompilerParams(
            dimension_semantics=("parallel", "arbitrary"), vmem_limit_bytes=VMEM_LIMIT),
        name="dsa",
    )(slopes, p, p, p, p, p, p, p)


def _diff_kernel(slopes_ref, lam_ref, qb_ref, kb_ref, vb_ref, zb_ref, g_ref, o_ref, base_ref, *scratch,
                 tq, tk):
    h = pl.program_id(0)
    qi = pl.program_id(1)
    q0 = qi * tq
    cs = slopes_ref[h] * LOG2E
    lane = lax.broadcasted_iota(jnp.int32, (tq, LANES), 1)
    qf = qb_ref[...].astype(F32) * (DIFF_DIM ** -0.5 * LOG2E)
    qs = jnp.concatenate([jnp.where(lane < DIFF_DIM, qf, 0.0),
                          jnp.where(lane >= DIFF_DIM, qf, 0.0)], axis=0).astype(qb_ref.dtype)
    row = lax.broadcasted_iota(jnp.int32, (2 * tq, tk), 0)
    row = jnp.where(row >= tq, row - tq, row)
    col = lax.broadcasted_iota(jnp.int32, (2 * tq, tk), 1)
    rc = row - col
    ones = jnp.ones((tk, LANES), vb_ref.dtype)

    @pl.when(qi == 0)
    def _():
        base_ref[...] = -cs * rc.astype(F32)

    def keys(kc):
        return pl.ds(pl.multiple_of(kc * tk, tk), tk)

    def qk(kc):
        return _nt_dot(qs, kb_ref[keys(kc), :]) + base_ref[...]

    def v1(kc):
        return jnp.concatenate([vb_ref[keys(kc), :], ones], axis=1)

    def shift(kc):
        return cs * (q0 - kc * tk).astype(F32)

    n_full = (q0 + 1) // tk
    valid = rc + (q0 - n_full * tk) >= 0
    _flash_chunks(n_full, qk, v1, shift, valid, *scratch)
    acc = scratch[-1][...]

    lv = lam_ref[...]
    lam = (jnp.exp(jnp.sum(lv[0:1] * lv[1:2], axis=1, keepdims=True))
           - jnp.exp(jnp.sum(lv[2:3] * lv[3:4], axis=1, keepdims=True)) + LAM_INIT)
    y = (acc[:tq, :LANES] / acc[:tq, LANES:]) - lam * (acc[tq:, :LANES] / acc[tq:, LANES:])
    ms = jnp.mean(y * y, axis=-1, keepdims=True)
    y = (y * lax.rsqrt(ms + RMS_EPS) * g_ref[...]) * (1.0 - LAM_INIT)
    o_ref[...] = (y * _silu(zb_ref[...].astype(F32))).astype(o_ref.dtype)


def _diff(p, slopes, lamv, g_subln, tq, tk):
    s = p.shape[0]
    nq = s // tq
    assert tk >= tq and s % tk == 0
    cb = lambda c: c // LANES
    kern = functools.partial(_diff_kernel, tq=tq, tk=tk)
    return pl.pallas_call(
        kern,
        out_shape=jax.ShapeDtypeStruct((s, WIDTH), CDT),
        grid=(N_HEADS_B, nq),
        in_specs=[
            pl.BlockSpec(memory_space=pltpu.SMEM),
            pl.BlockSpec((4, DIFF_DIM), lambda h, i: (0, 0)),
            pl.BlockSpec((tq, LANES), lambda h, i: (i, cb(C_QB) + h)),
            pl.BlockSpec((s, LANES), lambda h, i: (0, cb(C_KB) + h)),
            pl.BlockSpec((s, LANES), lambda h, i: (0, cb(C_VB) + h)),
            pl.BlockSpec((tq, LANES), lambda h, i: (i, cb(C_ZB) + h)),
            pl.BlockSpec((1, LANES), lambda h, i: (0, 0)),
        ],
        out_specs=pl.BlockSpec((tq, LANES), lambda h, i: (i, h)),
        scratch_shapes=[pltpu.VMEM((2 * tq, tk), F32)] + _flash_scratch(2 * tq, tk),
        compiler_params=pltpu.CompilerParams(
            dimension_semantics=("parallel", "arbitrary"), vmem_limit_bytes=VMEM_LIMIT),
        name="diff",
    )(slopes, lamv, p, p, p, p, g_subln)


def _memkv_kernel(mem_ref, g_ref, w_ref, o_ref, h_ref):
    @pl.when(pl.program_id(0) == 0)
    def _():
        xf = mem_ref[...]
        ms = jnp.mean(xf * xf, axis=-1, keepdims=True)
        h_ref[...] = (xf * lax.rsqrt(ms + RMS_EPS) * g_ref[...]).astype(h_ref.dtype)

    o_ref[...] = jnp.dot(h_ref[...], w_ref[...].astype(h_ref.dtype),
                         preferred_element_type=F32).astype(o_ref.dtype)


def _memkv(mem2, g, w, tn):
    nm, d = mem2.shape
    n = w.shape[1]
    return pl.pallas_call(
        _memkv_kernel,
        out_shape=jax.ShapeDtypeStruct((nm, n), CDT),
        grid=(n // tn,),
        in_specs=[
            pl.BlockSpec((nm, d), lambda j: (0, 0)),
            pl.BlockSpec((1, d), lambda j: (0, 0)),
            pl.BlockSpec((d, tn), lambda j: (0, j)),
        ],
        out_specs=pl.BlockSpec((nm, tn), lambda j: (0, j)),
        scratch_shapes=[pltpu.VMEM((nm, d), CDT)],
        compiler_params=pltpu.CompilerParams(
            dimension_semantics=("arbitrary",), vmem_limit_bytes=VMEM_LIMIT),
        name="memkv",
    )(mem2, g, w)


def _memattn_kernel(q_ref, z_ref, k_ref, v_ref, o_ref):
    s = _nt_dot(q_ref[...], k_ref[...]) * (MEM_HEAD_DIM ** -0.5)
    m = jnp.max(s, axis=1, keepdims=True)
    p = jnp.exp(s - m)
    l = jnp.sum(p, axis=1, keepdims=True)
    y = jnp.dot(p.astype(v_ref.dtype), v_ref[...], preferred_element_type=F32) / l
    o_ref[...] = (y * _silu(z_ref[...].astype(F32))).astype(o_ref.dtype)


def _memattn(p, mkv, tm):
    s = p.shape[0]
    nm = mkv.shape[0]
    hd = MEM_HEAD_DIM
    return pl.pallas_call(
        _memattn_kernel,
        out_shape=jax.ShapeDtypeStruct((s, WIDTH), CDT),
        grid=(s // tm, N_MEM_HEADS),
        in_specs=[
            pl.BlockSpec((tm, hd), lambda i, h: (i, C_QM // hd + h)),
            pl.BlockSpec((tm, hd), lambda i, h: (i, C_ZM // hd + h)),
            pl.BlockSpec((nm, hd), lambda i, h: (0, h)),
            pl.BlockSpec((nm, hd), lambda i, h: (0, N_MEM_HEADS + h)),
        ],
        out_specs=pl.BlockSpec((tm, hd), lambda i, h: (i, h)),
        compiler_params=pltpu.CompilerParams(
            dimension_semantics=("parallel", "parallel"), vmem_limit_bytes=VMEM_LIMIT),
        name="memattn",
    )(p, p, mkv, mkv)


def _merge_kernel(ua_ref, ub_ref, um_ref, ga_ref, gb_ref, gm_ref, b_ref, wa_ref, wb_ref, wm_ref, o_ref):
    d = o_ref.shape[1]
    acc = None
    for t, (u_ref, g_ref, w_ref) in enumerate(
            ((ua_ref, ga_ref, wa_ref), (ub_ref, gb_ref, wb_ref), (um_ref, gm_ref, wm_ref))):
        gate = jax.nn.sigmoid(g_ref[...].astype(F32) + b_ref[:, t * d:(t + 1) * d])
        term = gate * jnp.dot(u_ref[...], w_ref[...], preferred_element_type=F32)
        acc = term if acc is None else acc + term
    o_ref[...] = acc.astype(o_ref.dtype)


def _merge(ua, ub, um, p, b_gate, wa, wb, wm, tm):
    s = ua.shape[0]
    d = D_MODEL
    once = pl.Buffered(1)
    u_spec = pl.BlockSpec((tm, WIDTH), lambda i: (i, 0))
    w_spec = pl.BlockSpec((WIDTH, d), lambda i: (0, 0), pipeline_mode=once)
    return pl.pallas_call(
        _merge_kernel,
        out_shape=jax.ShapeDtypeStruct((s, d), CDT),
        grid=(s // tm,),
        in_specs=[
            u_spec, u_spec, u_spec,
            pl.BlockSpec((tm, d), lambda i: (i, C_GL // d + 0)),
            pl.BlockSpec((tm, d), lambda i: (i, C_GL // d + 1)),
            pl.BlockSpec((tm, d), lambda i: (i, C_GL // d + 2)),
            pl.BlockSpec((1, N_BRANCH * d), lambda i: (0, 0)),
            w_spec, w_spec, w_spec,
        ],
        out_specs=pl.BlockSpec((tm, d), lambda i: (i, 0)),
        compiler_params=pltpu.CompilerParams(
            dimension_semantics=("parallel",), vmem_limit_bytes=VMEM_LIMIT),
        name="merge",
    )(ua, ub, um, p, p, p, b_gate, wa, wb, wm)


def _out_kernel(x_ref, mg_ref, w_ref, g_ref, o_ref):
    y = x_ref[...] + jnp.dot(mg_ref[...], w_ref[...], preferred_element_type=F32)
    ms = jnp.mean(y * y, axis=-1, keepdims=True)
    o_ref[...] = (y * lax.rsqrt(ms + RMS_EPS) * g_ref[...]).astype(o_ref.dtype)


def _out(x2, merged, w_out, g_final, tm):
    s, d = x2.shape
    return pl.pallas_call(
        _out_kernel,
        out_shape=jax.ShapeDtypeStruct((s, d), x2.dtype),
        grid=(s // tm,),
        in_specs=[
            pl.BlockSpec((tm, d), lambda i: (i, 0)),
            pl.BlockSpec((tm, d), lambda i: (i, 0)),
            pl.BlockSpec((d, d), lambda i: (0, 0), pipeline_mode=pl.Buffered(1)),
            pl.BlockSpec((1, d), lambda i: (0, 0)),
        ],
        out_specs=pl.BlockSpec((tm, d), lambda i: (i, 0)),
        compiler_params=pltpu.CompilerParams(
            dimension_semantics=("parallel",), vmem_limit_bytes=VMEM_LIMIT),
        name="out",
    )(x2, merged, w_out, g_final)


def _column_moves():
    src = {}
    o = 0
    for name, width in (("qa", WIDTH), ("ka", WIDTH), ("va", WIDTH), ("za", WIDTH),
                        ("qi", N_IDX_HEADS * IDX_DIM), ("ki", IDX_DIM), ("wi", N_IDX_HEADS),
                        ("qb", WIDTH), ("kb", WIDTH), ("vb", WIDTH), ("zb", WIDTH),
                        ("qm", WIDTH), ("zm", WIDTH), ("gl", N_BRANCH * D_MODEL)):
        src[name] = (o, width)
        o += width
    dst = (("gl", C_GL), ("qi", C_QI), ("qa", C_QA), ("ka", C_KA), ("va", C_VA), ("za", C_ZA),
           ("qb", C_QB), ("kb", C_KB), ("vb", C_VB), ("zb", C_ZB), ("qm", C_QM), ("zm", C_ZM),
           ("ki", C_KK), ("ki", C_KK + IDX_DIM), ("wi", C_WI))
    return o, [(d,) + src[name] for name, d in dst]


def _reorder_w_in_t(w):
    d_in, moves = _column_moves()
    assert w.shape[1] == d_in
    wt = jnp.swapaxes(w, 0, 1)
    pieces = [wt[s0:s0 + width] for _, s0, width in sorted(moves)]
    pieces.append(jnp.zeros((D_PAD - C_WI - N_IDX_HEADS, w.shape[0]), w.dtype))
    out = jnp.concatenate(pieces, axis=0)
    assert out.shape[0] == D_PAD
    return out.astype(CDT)


def kernel(x, mem, g_in, w_in, b_gate, lam_q1, lam_k1, lam_q2, lam_k2, g_subln, g_mem,
           w_mem_kv, w_br_a, w_br_b, w_br_m, w_out, g_final):
    bsz, s_len, d = x.shape
    assert bsz == 1 and d == D_MODEL and g_in.shape[0] == 1
    topk = min(TOPK_MAX, s_len // 4)
    x2 = x.reshape(s_len, d)
    slopes = 2.0 ** (-8.0 * jnp.arange(1, N_HEADS_A + 1, dtype=F32) / N_HEADS_A)

    p = _proj(x2, g_in[0].reshape(1, d), _reorder_w_in_t(w_in[0]), tm=512, tn=768)
    ua = _dsa(p, slopes, tq=512, tk=512, topk=topk)
    lamv = jnp.stack([lam_q1[0], lam_k1[0], lam_q2[0], lam_k2[0]]).astype(F32)
    ub = _diff(p, slopes, lamv, g_subln[0].reshape(1, 2 * DIFF_DIM), tq=256, tk=512)
    mkv = _memkv(mem.reshape(-1, d), g_mem[0].reshape(1, d), w_mem_kv[0], tn=512)
    um = _memattn(p, mkv, tm=1024)
    merged = _merge(ua, ub, um, p, b_gate[0].reshape(1, -1),
                    w_br_a[0].astype(CDT), w_br_b[0].astype(CDT), w_br_m[0].astype(CDT), tm=512)
    y = _out(x2, merged, w_out[0].astype(CDT), g_final.reshape(1, d), tm=512)
    return y.reshape(bsz, s_len, d)
```

```python
import functools

import jax
import jax.numpy as jnp
from jax import lax
from jax.experimental import pallas as pl
from jax.experimental.pallas import tpu as pltpu

F32 = jnp.float32
CDT = jnp.bfloat16

D_MODEL = 2048
N_HEADS_A = 8
HEAD_DIM_A = 128
N_IDX_HEADS = 16
IDX_DIM = 64
TOPK_MAX = 256
N_HEADS_B = 8
DIFF_DIM = 64
N_MEM_HEADS = 4
MEM_HEAD_DIM = 256
N_BRANCH = 3
WIDTH = 1024
RMS_EPS = 1e-6
NEG_INF = -1e30
F32_BIG = 3.0e38
LAM_INIT = 0.2
LANES = 128
LOG2E = 1.4426950408889634

C_GL = 0
C_QI = 6144
C_QA = 7168
C_KA = 8192
C_VA = 9216
C_ZA = 10240
C_QB = 11264
C_KB = 12288
C_VB = 13312
C_ZB = 14336
C_QM = 15360
C_ZM = 16384
C_KK = 17408
C_WI = 17536
D_PAD = 17664

VMEM_LIMIT = 56 * 1024 * 1024

BISECT_MAX_ITERS = 64
NORM_ROWS = 128
SOFT_ROWS = 16


def _nt_dot(a, b):
    return lax.dot_general(a, b, (((1,), (1,)), ((), ())), preferred_element_type=F32)


def _silu(z):
    return z * jax.nn.sigmoid(z)


def _proj_kernel(x_ref, g_ref, w_ref, o_ref, h_ref):
    @pl.when(pl.program_id(1) == 0)
    def _():
        for r in range(0, x_ref.shape[0], NORM_ROWS):
            xf = x_ref[r:r + NORM_ROWS, :]
            ms = jnp.mean(xf * xf, axis=-1, keepdims=True)
            h_ref[r:r + NORM_ROWS, :] = (xf * lax.rsqrt(ms + RMS_EPS) * g_ref[...]).astype(h_ref.dtype)

    o_ref[...] = _nt_dot(h_ref[...], w_ref[...]).astype(o_ref.dtype)


def _proj(x2, g, wt, tm, tn):
    s, d = x2.shape
    n = wt.shape[0]
    return pl.pallas_call(
        _proj_kernel,
        out_shape=jax.ShapeDtypeStruct((s, n), CDT),
        grid=(s // tm, n // tn),
        in_specs=[
            pl.BlockSpec((tm, d), lambda i, j: (i, 0)),
            pl.BlockSpec((1, d), lambda i, j: (0, 0)),
            pl.BlockSpec((tn, d), lambda i, j: (j, 0)),
        ],
        out_specs=pl.BlockSpec((tm, tn), lambda i, j: (i, j)),
        scratch_shapes=[pltpu.VMEM((tm, d), CDT)],
        compiler_params=pltpu.CompilerParams(
            dimension_semantics=("parallel", "arbitrary"), vmem_limit_bytes=VMEM_LIMIT),
        name="proj",
    )(x2, g, wt)


def _tile_lanes(x, n):
    return x if n == 1 else jnp.concatenate([x] * n, axis=1)


def _flash_scratch(rows, tk):
    return [pltpu.VMEM((rows, tk), F32), pltpu.VMEM((rows, tk), F32),
            pltpu.VMEM((rows, tk), CDT), pltpu.VMEM((rows, tk), CDT),
            pltpu.VMEM((rows, LANES), F32), pltpu.VMEM((rows, LANES), F32),
            pltpu.VMEM((rows, LANES), F32), pltpu.VMEM((rows, 2 * LANES), F32)]


def _flash_chunks(n_full, qk, v1, shift, last_valid, s_a, s_b, p_a, p_b, al_a, al_b, m_ref, acc_ref):
    tk = s_a.shape[1]

    def soft(s_ref, p_ref, al_ref, c, valid=None):
        d = shift(c)
        for g in range(s_ref.shape[0] // SOFT_ROWS):
            rows = slice(g * SOFT_ROWS, (g + 1) * SOFT_ROWS)
            s = s_ref[rows, :]
            if valid is not None:
                s = jnp.where(valid(rows), s, NEG_INF)
            m = m_ref[rows, :]
            m_new = jnp.maximum(m, jnp.max(s, axis=1, keepdims=True) - d)
            al_ref[rows, :] = jnp.exp2(m - m_new)
            m_ref[rows, :] = m_new
            p_ref[rows, :] = jnp.exp2(s - _tile_lanes(m_new + d, tk // LANES)).astype(p_ref.dtype)

    def pv(p_ref, al_ref, c):
        acc_ref[...] = (_tile_lanes(al_ref[...], 2) * acc_ref[...]
                        + jnp.dot(p_ref[...], v1(c), preferred_element_type=F32))

    m_ref[...] = jnp.full(m_ref.shape, NEG_INF, F32)
    acc_ref[...] = jnp.zeros(acc_ref.shape, F32)
    s_a[...] = qk(0)
    p_b[...] = jnp.zeros(p_b.shape, p_b.dtype)
    al_b[...] = jnp.ones(al_b.shape, F32)

    def pair(c0):
        pv(p_b, al_b, jnp.maximum(c0 - 1, 0))
        s_b[...] = qk(c0 + 1)
        soft(s_a, p_a, al_a, c0)
        pv(p_a, al_a, c0)
        s_a[...] = qk(c0 + 2)
        soft(s_b, p_b, al_b, c0 + 1)

    def quad_body(j, _):
        pair(4 * j)
        pair(4 * j + 2)
        return 0

    def pair_body(j, _):
        pair(4 * (n_full // 4) + 2 * j)
        return 0

    lax.fori_loop(0, n_full // 4, quad_body, 0)
    lax.fori_loop(0, (n_full % 4) // 2, pair_body, 0)

    @pl.when(n_full % 2 == 1)
    def _():
        c0 = n_full - 1
        pv(p_b, al_b, jnp.maximum(c0 - 1, 0))
        s_b[...] = qk(n_full)
        soft(s_a, p_a, al_a, c0)
        pv(p_a, al_a, c0)
        soft(s_b, p_b, al_b, n_full, last_valid)
        pv(p_b, al_b, n_full)

    @pl.when(n_full % 2 == 0)
    def _():
        pv(p_b, al_b, jnp.maximum(n_full - 1, 0))
        soft(s_a, p_a, al_a, n_full, last_valid)
        pv(p_a, al_a, n_full)


def _dsa_kernel(slopes_ref, qa_ref, ka_ref, va_ref, za_ref, qi_ref, kk_ref, wi_ref, o_ref,
                sc_ref, qz_ref, wb_ref, *scratch, tq, tk, topk):
    qi = pl.program_id(0)
    h = pl.program_id(1)
    q0 = qi * tq
    n_full = (q0 + 1) // tk
    row = lax.broadcasted_iota(jnp.int32, (tq, tk), 0)
    col = lax.broadcasted_iota(jnp.int32, (tq, tk), 1)
    rc = row - col
    nsub = tk // LANES
    rblk = 128

    def keys(kc):
        return pl.ds(pl.multiple_of(kc * tk, tk), tk)

    def fold(a, op):
        r = a[:, :LANES]
        for t in range(1, nsub):
            r = op(r, a[:, t * LANES:(t + 1) * LANES])
        return r

    @pl.when(h == 0)
    def _select():
        lane = lax.broadcasted_iota(jnp.int32, (tq, LANES), 1)
        for m in range(N_IDX_HEADS // 2):
            qp = qi_ref[:, m * LANES:(m + 1) * LANES].astype(F32)
            qz_ref[2 * m] = jnp.where(lane < IDX_DIM, qp, 0.0).astype(qz_ref.dtype)
            qz_ref[2 * m + 1] = jnp.where(lane >= IDX_DIM, qp, 0.0).astype(qz_ref.dtype)
        wi = wi_ref[...].astype(F32) * (IDX_DIM ** -0.5 * N_IDX_HEADS ** -0.5)
        for j in range(N_IDX_HEADS):
            wb_ref[j] = jnp.broadcast_to(wi[:, j:j + 1], (tq, LANES))

        def chunk_scores(kc):
            kk = kk_ref[keys(kc), :]
            acc = jnp.zeros((tq, tk), F32)
            for j in range(N_IDX_HEADS):
                acc = acc + jnp.maximum(_nt_dot(qz_ref[j], kk), 0.0) * _tile_lanes(wb_ref[j], nsub)
            return acc

        def full_body(kc, carry):
            rmin, rmax = carry
            a = chunk_scores(kc)
            sc_ref[kc] = a
            return jnp.minimum(rmin, fold(a, jnp.minimum)), jnp.maximum(rmax, fold(a, jnp.maximum))

        rmin, rmax = lax.fori_loop(
            0, n_full, full_body,
            (jnp.full((tq, LANES), F32_BIG, F32), jnp.full((tq, LANES), -F32_BIG, F32)))
        a = chunk_scores(n_full)
        causal = rc + (q0 - n_full * tk) >= 0
        sc_ref[n_full] = jnp.where(causal, a, -F32_BIG)
        rmin = jnp.minimum(rmin, fold(jnp.where(causal, a, F32_BIG), jnp.minimum))
        rmax = jnp.maximum(rmax, fold(jnp.where(causal, a, -F32_BIG), jnp.maximum))
        lo0 = jnp.min(rmin, axis=1, keepdims=True)
        hi0 = jnp.max(rmax, axis=1, keepdims=True)

        sc_ref[n_full + 1] = jnp.full((tq, tk), -F32_BIG, F32)
        lane_ones = jnp.ones((LANES, LANES), CDT)

        def count_ge(tb):
            cnts = []
            for r in range(tq // rblk):
                groups = [slice(r * rblk + g * 8, r * rblk + (g + 1) * 8) for g in range(rblk // 8)]
                tts = [tb[g] for g in groups]

                def cbody(j, cnt, groups=groups, tts=tts):
                    parts = []
                    for i, (g, t) in enumerate(zip(groups, tts)):
                        c = cnt[i * 8:(i + 1) * 8]
                        for u in range(2):
                            x = sc_ref[2 * j + u, g, :]
                            for l in range(nsub):
                                c = c + jnp.where(x[:, l * LANES:(l + 1) * LANES] >= t, 1.0, 0.0)
                        parts.append(c)
                    return jnp.concatenate(parts, axis=0)

                cnts.append(lax.fori_loop(0, (n_full + 2) // 2, cbody, jnp.zeros((rblk, LANES), F32)))
            cnt = jnp.concatenate(cnts, axis=0).astype(CDT)
            return jnp.dot(cnt, lane_ones, preferred_element_type=F32)

        kf = float(topk)
        n_causal = (q0 + row[:, :LANES] + 1).astype(F32)
        lo0 = jnp.broadcast_to(lo0, (tq, LANES))
        hi0 = jnp.broadcast_to(hi0, (tq, LANES))

        def cond(c):
            return jnp.logical_and(c[0] < BISECT_MAX_ITERS, c[4])

        def body(c):
            it, lo, hi, cnt_lo, _ = c
            mid = lo + 0.5 * (hi - lo)
            cm = count_ge(mid)
            ge = cm >= kf
            moving = jnp.logical_and(mid > lo, mid < hi)
            lo = jnp.where(ge, mid, lo)
            hi = jnp.where(ge, hi, mid)
            cnt_lo = jnp.where(ge, cm, cnt_lo)
            active = jnp.logical_and(cnt_lo > kf, moving)
            flag = jnp.max(jnp.where(active, 1.0, 0.0)) > 0.0
            return it + 1, lo, hi, cnt_lo, flag

        c0 = (jnp.int32(0), lo0, hi0, n_causal, jnp.max(n_causal) > kf)
        _, lo, _, _, _ = lax.while_loop(cond, body, c0)
        lob = _tile_lanes(lo, nsub)

        def mbody(kc, _):
            sc_ref[kc] = jnp.where(sc_ref[kc] >= lob, 0.0, NEG_INF)
            return 0

        lax.fori_loop(0, n_full + 1, mbody, 0)

    cs = slopes_ref[h] * LOG2E
    q = (qa_ref[...].astype(F32) * (HEAD_DIM_A ** -0.5 * LOG2E)).astype(qa_ref.dtype)
    colbias = cs * lax.broadcasted_iota(jnp.int32, (1, tk), 1).astype(F32)
    ones = jnp.ones((tk, LANES), va_ref.dtype)

    def qk(kc):
        return _nt_dot(q, ka_ref[keys(kc), :]) + colbias + sc_ref[kc]

    def v1(kc):
        return jnp.concatenate([va_ref[keys(kc), :], ones], axis=1)

    def shift(kc):
        return cs * (q0 - kc * tk).astype(F32)

    _flash_chunks(n_full, qk, v1, shift, None, *scratch)
    acc = scratch[-1][...]
    y = acc[:, :LANES] / acc[:, LANES:]
    o_ref[...] = (y * _silu(za_ref[...].astype(F32))).astype(o_ref.dtype)


def _dsa(p, slopes, tq, tk, topk):
    s = p.shape[0]
    nq = s // tq
    assert tk >= tq and s % tk == 0
    cb = lambda c: c // LANES
    kern = functools.partial(_dsa_kernel, tq=tq, tk=tk, topk=topk)
    return pl.pallas_call(
        kern,
        out_shape=jax.ShapeDtypeStruct((s, WIDTH), CDT),
        grid=(nq, N_HEADS_A),
        in_specs=[
            pl.BlockSpec(memory_space=pltpu.SMEM),
            pl.BlockSpec((tq, LANES), lambda i, h: (i, cb(C_QA) + h)),
            pl.BlockSpec((s, LANES), lambda i, h: (0, cb(C_KA) + h)),
            pl.BlockSpec((s, LANES), lambda i, h: (0, cb(C_VA) + h)),
            pl.BlockSpec((tq, LANES), lambda i, h: (i, cb(C_ZA) + h)),
            pl.BlockSpec((tq, WIDTH), lambda i, h: (i, C_QI // WIDTH)),
            pl.BlockSpec((s, LANES), lambda i, h: (0, cb(C_KK))),
            pl.BlockSpec((tq, LANES), lambda i, h: (i, cb(C_WI))),
        ],
        out_specs=pl.BlockSpec((tq, LANES), lambda i, h: (i, h)),
        scratch_shapes=[
            pltpu.VMEM((s // tk + 1, tq, tk), F32),
            pltpu.VMEM((N_IDX_HEADS, tq, LANES), CDT),
            pltpu.VMEM((N_IDX_HEADS, tq, LANES), F32),
        ] + _flash_scratch(tq, tk),
        compiler_params=pltpu.CompilerParams(
            dimension_semantics=("parallel", "arbitrary"), vmem_limit_bytes=VMEM_LIMIT),
        name="dsa",
    )(slopes, p, p, p, p, p, p, p)


def _diff_kernel(slopes_ref, lam_ref, qb_ref, kb_ref, vb_ref, zb_ref, g_ref, o_ref, *scratch, tq, tk):
    h = pl.program_id(0)
    qi = pl.program_id(1)
    q0 = qi * tq
    cs = slopes_ref[h] * LOG2E
    lane = lax.broadcasted_iota(jnp.int32, (tq, LANES), 1)
    qf = qb_ref[...].astype(F32) * (DIFF_DIM ** -0.5 * LOG2E)
    qs = jnp.concatenate([jnp.where(lane < DIFF_DIM, qf, 0.0),
                          jnp.where(lane >= DIFF_DIM, qf, 0.0)], axis=0).astype(qb_ref.dtype)
    colbias = cs * lax.broadcasted_iota(jnp.int32, (1, tk), 1).astype(F32)
    ones = jnp.ones((tk, LANES), vb_ref.dtype)

    def keys(kc):
        return pl.ds(pl.multiple_of(kc * tk, tk), tk)

    def qk(kc):
        return _nt_dot(qs, kb_ref[keys(kc), :]) + colbias

    def v1(kc):
        return jnp.concatenate([vb_ref[keys(kc), :], ones], axis=1)

    def shift(kc):
        return cs * (q0 - kc * tk).astype(F32)

    n_full = (q0 + 1) // tk

    def valid(rows):
        r = lax.broadcasted_iota(jnp.int32, (SOFT_ROWS, tk), 0) + rows.start % tq
        c = lax.broadcasted_iota(jnp.int32, (SOFT_ROWS, tk), 1)
        return r - c + (q0 - n_full * tk) >= 0

    _flash_chunks(n_full, qk, v1, shift, valid, *scratch)
    acc = scratch[-1][...]

    lv = lam_ref[...]
    lam = (jnp.exp(jnp.sum(lv[0:1] * lv[1:2], axis=1, keepdims=True))
           - jnp.exp(jnp.sum(lv[2:3] * lv[3:4], axis=1, keepdims=True)) + LAM_INIT)
    y = (acc[:tq, :LANES] / acc[:tq, LANES:]) - lam * (acc[tq:, :LANES] / acc[tq:, LANES:])
    ms = jnp.mean(y * y, axis=-1, keepdims=True)
    y = (y * lax.rsqrt(ms + RMS_EPS) * g_ref[...]) * (1.0 - LAM_INIT)
    o_ref[...] = (y * _silu(zb_ref[...].astype(F32))).astype(o_ref.dtype)


def _diff(p, slopes, lamv, g_subln, tq, tk):
    s = p.shape[0]
    nq = s // tq
    assert tk >= tq and s % tk == 0
    cb = lambda c: c // LANES
    kern = functools.partial(_diff_kernel, tq=tq, tk=tk)
    return pl.pallas_call(
        kern,
        out_shape=jax.ShapeDtypeStruct((s, WIDTH), CDT),
        grid=(N_HEADS_B, nq),
        in_specs=[
            pl.BlockSpec(memory_space=pltpu.SMEM),
            pl.BlockSpec((4, DIFF_DIM), lambda h, i: (0, 0)),
            pl.BlockSpec((tq, LANES), lambda h, i: (i, cb(C_QB) + h)),
            pl.BlockSpec((s, LANES), lambda h, i: (0, cb(C_KB) + h)),
            pl.BlockSpec((s, LANES), lambda h, i: (0, cb(C_VB) + h)),
            pl.BlockSpec((tq, LANES), lambda h, i: (i, cb(C_ZB) + h)),
            pl.BlockSpec((1, LANES), lambda h, i: (0, 0)),
        ],
        out_specs=pl.BlockSpec((tq, LANES), lambda h, i: (i, h)),
        scratch_shapes=_flash_scratch(2 * tq, tk),
        compiler_params=pltpu.CompilerParams(
            dimension_semantics=("parallel", "parallel"), vmem_limit_bytes=VMEM_LIMIT),
        name="diff",
    )(slopes, lamv, p, p, p, p, g_subln)


def _memkv_kernel(mem_ref, g_ref, w_ref, o_ref, h_ref):
    @pl.when(pl.program_id(0) == 0)
    def _():
        xf = mem_ref[...]
        ms = jnp.mean(xf * xf, axis=-1, keepdims=True)
        h_ref[...] = (xf * lax.rsqrt(ms + RMS_EPS) * g_ref[...]).astype(h_ref.dtype)

    o_ref[...] = jnp.dot(h_ref[...], w_ref[...].astype(h_ref.dtype),
                         preferred_element_type=F32).astype(o_ref.dtype)


def _memkv(mem2, g, w, tn):
    nm, d = mem2.shape
    n = w.shape[1]
    return pl.pallas_call(
        _memkv_kernel,
        out_shape=jax.ShapeDtypeStruct((nm, n), CDT),
        grid=(n // tn,),
        in_specs=[
            pl.BlockSpec((nm, d), lambda j: (0, 0)),
            pl.BlockSpec((1, d), lambda j: (0, 0)),
            pl.BlockSpec((d, tn), lambda j: (0, j)),
        ],
        out_specs=pl.BlockSpec((nm, tn), lambda j: (0, j)),
        scratch_shapes=[pltpu.VMEM((nm, d), CDT)],
        compiler_params=pltpu.CompilerParams(
            dimension_semantics=("arbitrary",), vmem_limit_bytes=VMEM_LIMIT),
        name="memkv",
    )(mem2, g, w)


def _memattn_kernel(q_ref, z_ref, k_ref, v_ref, o_ref):
    s = _nt_dot(q_ref[...], k_ref[...]) * (MEM_HEAD_DIM ** -0.5)
    m = jnp.max(s, axis=1, keepdims=True)
    p = jnp.exp(s - m)
    l = jnp.sum(p, axis=1, keepdims=True)
    y = jnp.dot(p.astype(v_ref.dtype), v_ref[...], preferred_element_type=F32) / l
    o_ref[...] = (y * _silu(z_ref[...].astype(F32))).astype(o_ref.dtype)


def _memattn(p, mkv, tm):
    s = p.shape[0]
    nm = mkv.shape[0]
    hd = MEM_HEAD_DIM
    return pl.pallas_call(
        _memattn_kernel,
        out_shape=jax.ShapeDtypeStruct((s, WIDTH), CDT),
        grid=(s // tm, N_MEM_HEADS),
        in_specs=[
            pl.BlockSpec((tm, hd), lambda i, h: (i, C_QM // hd + h)),
            pl.BlockSpec((tm, hd), lambda i, h: (i, C_ZM // hd + h)),
            pl.BlockSpec((nm, hd), lambda i, h: (0, h)),
            pl.BlockSpec((nm, hd), lambda i, h: (0, N_MEM_HEADS + h)),
        ],
        out_specs=pl.BlockSpec((tm, hd), lambda i, h: (i, h)),
        compiler_params=pltpu.CompilerParams(
            dimension_semantics=("parallel", "parallel"), vmem_limit_bytes=VMEM_LIMIT),
        name="memattn",
    )(p, p, mkv, mkv)


def _merge_kernel(ua_ref, ub_ref, um_ref, ga_ref, gb_ref, gm_ref, b_ref, wa_ref, wb_ref, wm_ref, o_ref):
    d = o_ref.shape[1]
    acc = None
    for t, (u_ref, g_ref, w_ref) in enumerate(
            ((ua_ref, ga_ref, wa_ref), (ub_ref, gb_ref, wb_ref), (um_ref, gm_ref, wm_ref))):
        gate = jax.nn.sigmoid(g_ref[...].astype(F32) + b_ref[:, t * d:(t + 1) * d])
        term = gate * jnp.dot(u_ref[...], w_ref[...], preferred_element_type=F32)
        acc = term if acc is None else acc + term
    o_ref[...] = acc.astype(o_ref.dtype)


def _merge(ua, ub, um, p, b_gate, wa, wb, wm, tm):
    s = ua.shape[0]
    d = D_MODEL
    once = pl.Buffered(1)
    u_spec = pl.BlockSpec((tm, WIDTH), lambda i: (i, 0))
    w_spec = pl.BlockSpec((WIDTH, d), lambda i: (0, 0), pipeline_mode=once)
    return pl.pallas_call(
        _merge_kernel,
        out_shape=jax.ShapeDtypeStruct((s, d), CDT),
        grid=(s // tm,),
        in_specs=[
            u_spec, u_spec, u_spec,
            pl.BlockSpec((tm, d), lambda i: (i, C_GL // d + 0)),
            pl.BlockSpec((tm, d), lambda i: (i, C_GL // d + 1)),
            pl.BlockSpec((tm, d), lambda i: (i, C_GL // d + 2)),
            pl.BlockSpec((1, N_BRANCH * d), lambda i: (0, 0)),
            w_spec, w_spec, w_spec,
        ],
        out_specs=pl.BlockSpec((tm, d), lambda i: (i, 0)),
        compiler_params=pltpu.CompilerParams(
            dimension_semantics=("parallel",), vmem_limit_bytes=VMEM_LIMIT),
        name="merge",
    )(ua, ub, um, p, p, p, b_gate, wa, wb, wm)


def _out_kernel(x_ref, mg_ref, w_ref, g_ref, o_ref):
    y = x_ref[...] + jnp.dot(mg_ref[...], w_ref[...], preferred_element_type=F32)
    ms = jnp.mean(y * y, axis=-1, keepdims=True)
    o_ref[...] = (y * lax.rsqrt(ms + RMS_EPS) * g_ref[...]).astype(o_ref.dtype)


def _out(x2, merged, w_out, g_final, tm):
    s, d = x2.shape
    return pl.pallas_call(
        _out_kernel,
        out_shape=jax.ShapeDtypeStruct((s, d), x2.dtype),
        grid=(s // tm,),
        in_specs=[
            pl.BlockSpec((tm, d), lambda i: (i, 0)),
            pl.BlockSpec((tm, d), lambda i: (i, 0)),
            pl.BlockSpec((d, d), lambda i: (0, 0), pipeline_mode=pl.Buffered(1)),
            pl.BlockSpec((1, d), lambda i: (0, 0)),
        ],
        out_specs=pl.BlockSpec((tm, d), lambda i: (i, 0)),
        compiler_params=pltpu.CompilerParams(
            dimension_semantics=("parallel",), vmem_limit_bytes=VMEM_LIMIT),
        name="out",
    )(x2, merged, w_out, g_final)


def _column_moves():
    src = {}
    o = 0
    for name, width in (("qa", WIDTH), ("ka", WIDTH), ("va", WIDTH), ("za", WIDTH),
                        ("qi", N_IDX_HEADS * IDX_DIM), ("ki", IDX_DIM), ("wi", N_IDX_HEADS),
                        ("qb", WIDTH), ("kb", WIDTH), ("vb", WIDTH), ("zb", WIDTH),
                        ("qm", WIDTH), ("zm", WIDTH), ("gl", N_BRANCH * D_MODEL)):
        src[name] = (o, width)
        o += width
    dst = (("gl", C_GL), ("qi", C_QI), ("qa", C_QA), ("ka", C_KA), ("va", C_VA), ("za", C_ZA),
           ("qb", C_QB), ("kb", C_KB), ("vb", C_VB), ("zb", C_ZB), ("qm", C_QM), ("zm", C_ZM),
           ("ki", C_KK), ("ki", C_KK + IDX_DIM), ("wi", C_WI))
    return o, [(d,) + src[name] for name, d in dst]


def _reorder_w_in_t(w):
    d_in, moves = _column_moves()
    assert w.shape[1] == d_in
    wt = jnp.swapaxes(w, 0, 1)
    pieces = [wt[s0:s0 + width] for _, s0, width in sorted(moves)]
    pieces.append(jnp.zeros((D_PAD - C_WI - N_IDX_HEADS, w.shape[0]), w.dtype))
    out = jnp.concatenate(pieces, axis=0)
    assert out.shape[0] == D_PAD
    return out.astype(CDT)


def kernel(x, mem, g_in, w_in, b_gate, lam_q1, lam_k1, lam_q2, lam_k2, g_subln, g_mem,
           w_mem_kv, w_br_a, w_br_b, w_br_m, w_out, g_final):
    bsz, s_len, d = x.shape
    assert bsz == 1 and d == D_MODEL and g_in.shape[0] == 1
    topk = min(TOPK_MAX, s_len // 4)
    x2 = x.reshape(s_len, d)
    slopes = 2.0 ** (-8.0 * jnp.arange(1, N_HEADS_A + 1, dtype=F32) / N_HEADS_A)

    p = _proj(x2, g_in[0].reshape(1, d), _reorder_w_in_t(w_in[0]), tm=1024, tn=768)
    ua = _dsa(p, slopes, tq=512, tk=512, topk=topk)
    lamv = jnp.stack([lam_q1[0], lam_k1[0], lam_q2[0], lam_k2[0]]).astype(F32)
    ub = _diff(p, slopes, lamv, g_subln[0].reshape(1, 2 * DIFF_DIM), tq=512, tk=512)
    mkv = _memkv(mem.reshape(-1, d), g_mem[0].reshape(1, d), w_mem_kv[0], tn=512)
    um = _memattn(p, mkv, tm=1024)
    merged = _merge(ua, ub, um, p, b_gate[0].reshape(1, -1),
                    w_br_a[0].astype(CDT), w_br_b[0].astype(CDT), w_br_m[0].astype(CDT), tm=512)
    y = _out(x2, merged, w_out[0].astype(CDT), g_final.reshape(1, d), tm=512)
    return y.reshape(bsz, s_len, d)
```

```python
import functools

import jax
import jax.numpy as jnp
from jax import lax
from jax.experimental import pallas as pl
from jax.experimental.pallas import tpu as pltpu

F32 = jnp.float32
CDT = jnp.bfloat16

D_MODEL = 2048
N_HEADS_A = 8
HEAD_DIM_A = 128
N_IDX_HEADS = 16
IDX_DIM = 64
TOPK_MAX = 256
N_HEADS_B = 8
DIFF_DIM = 64
N_MEM_HEADS = 4
MEM_HEAD_DIM = 256
N_BRANCH = 3
WIDTH = 1024
RMS_EPS = 1e-6
NEG_INF = -1e30
F32_BIG = 3.0e38
LAM_INIT = 0.2
LANES = 128
LOG2E = 1.4426950408889634

C_GL = 0
C_QI = 6144
C_QA = 7168
C_KA = 8192
C_VA = 9216
C_ZA = 10240
C_QB = 11264
C_KB = 12288
C_VB = 13312
C_ZB = 14336
C_QM = 15360
C_ZM = 16384
C_KK = 17408
C_WI = 17536
D_PAD = 17664

VMEM_LIMIT = 56 * 1024 * 1024

BISECT_MAX_ITERS = 96
FLASH_PAIRS_PER_TRIP = 2
NORM_ROWS = 128
SOFT_ROWS = 16


def _nt_dot(a, b):
    return lax.dot_general(a, b, (((1,), (1,)), ((), ())), preferred_element_type=F32)


def _silu(z):
    return z * jax.nn.sigmoid(z)


def _proj_kernel(x_ref, g_ref, w_ref, o_ref, h_ref):
    @pl.when(pl.program_id(1) == 0)
    def _():
        for r in range(0, x_ref.shape[0], NORM_ROWS):
            xf = x_ref[r:r + NORM_ROWS, :]
            ms = jnp.mean(xf * xf, axis=-1, keepdims=True)
            h_ref[r:r + NORM_ROWS, :] = (xf * lax.rsqrt(ms + RMS_EPS) * g_ref[...]).astype(h_ref.dtype)

    o_ref[...] = _nt_dot(h_ref[...], w_ref[...]).astype(o_ref.dtype)


def _proj(x2, g, wt, tm, tn):
    s, d = x2.shape
    n = wt.shape[0]
    return pl.pallas_call(
        _proj_kernel,
        out_shape=jax.ShapeDtypeStruct((s, n), CDT),
        grid=(s // tm, n // tn),
        in_specs=[
            pl.BlockSpec((tm, d), lambda i, j: (i, 0)),
            pl.BlockSpec((1, d), lambda i, j: (0, 0)),
            pl.BlockSpec((tn, d), lambda i, j: (j, 0)),
        ],
        out_specs=pl.BlockSpec((tm, tn), lambda i, j: (i, j)),
        scratch_shapes=[pltpu.VMEM((tm, d), CDT)],
        compiler_params=pltpu.CompilerParams(
            dimension_semantics=("parallel", "arbitrary"), vmem_limit_bytes=VMEM_LIMIT),
        name="proj",
    )(x2, g, wt)


def _tile_lanes(x, n):
    return x if n == 1 else jnp.concatenate([x] * n, axis=1)


def _flash_scratch(rows, tk):
    return [pltpu.VMEM((rows, tk), F32), pltpu.VMEM((rows, tk), F32),
            pltpu.VMEM((rows, tk), CDT), pltpu.VMEM((rows, tk), CDT),
            pltpu.VMEM((rows, LANES), F32), pltpu.VMEM((rows, LANES), F32),
            pltpu.VMEM((rows, LANES), F32), pltpu.VMEM((rows, 2 * LANES), F32)]


def _flash_chunks(n_full, qk, v1, shift, last_valid, s_a, s_b, p_a, p_b, al_a, al_b, m_ref, acc_ref):
    tk = s_a.shape[1]

    def soft(s_ref, p_ref, al_ref, c, valid=None):
        d = shift(c)
        for g in range(s_ref.shape[0] // SOFT_ROWS):
            rows = slice(g * SOFT_ROWS, (g + 1) * SOFT_ROWS)
            s = s_ref[rows, :]
            if valid is not None:
                s = jnp.where(valid(rows), s, NEG_INF)
            m = m_ref[rows, :]
            m_new = jnp.maximum(m, jnp.max(s, axis=1, keepdims=True) - d)
            al_ref[rows, :] = jnp.exp2(m - m_new)
            m_ref[rows, :] = m_new
            p_ref[rows, :] = jnp.exp2(s - _tile_lanes(m_new + d, tk // LANES)).astype(p_ref.dtype)

    def pv(p_ref, al_ref, c):
        acc_ref[...] = (_tile_lanes(al_ref[...], 2) * acc_ref[...]
                        + jnp.dot(p_ref[...], v1(c), preferred_element_type=F32))

    m_ref[...] = jnp.full(m_ref.shape, NEG_INF, F32)
    acc_ref[...] = jnp.zeros(acc_ref.shape, F32)
    s_a[...] = qk(0)
    p_b[...] = jnp.zeros(p_b.shape, p_b.dtype)
    al_b[...] = jnp.ones(al_b.shape, F32)

    def pair(c0):
        pv(p_b, al_b, jnp.maximum(c0 - 1, 0))
        s_b[...] = qk(c0 + 1)
        soft(s_a, p_a, al_a, c0)
        pv(p_a, al_a, c0)
        s_a[...] = qk(c0 + 2)
        soft(s_b, p_b, al_b, c0 + 1)

    span = 2 * FLASH_PAIRS_PER_TRIP

    def multi_body(j, _):
        for u in range(FLASH_PAIRS_PER_TRIP):
            pair(span * j + 2 * u)
        return 0

    def pair_body(j, _):
        pair(span * (n_full // span) + 2 * j)
        return 0

    lax.fori_loop(0, n_full // span, multi_body, 0)
    lax.fori_loop(0, (n_full % span) // 2, pair_body, 0)

    @pl.when(n_full % 2 == 1)
    def _():
        c0 = n_full - 1
        pv(p_b, al_b, jnp.maximum(c0 - 1, 0))
        s_b[...] = qk(n_full)
        soft(s_a, p_a, al_a, c0)
        pv(p_a, al_a, c0)
        soft(s_b, p_b, al_b, n_full, last_valid)
        pv(p_b, al_b, n_full)

    @pl.when(n_full % 2 == 0)
    def _():
        pv(p_b, al_b, jnp.maximum(n_full - 1, 0))
        soft(s_a, p_a, al_a, n_full, last_valid)
        pv(p_a, al_a, n_full)


def _dsa_kernel(slopes_ref, qa_ref, ka_ref, va_ref, za_ref, qi_ref, kk_ref, wi_ref, o_ref,
                sc_ref, qz_ref, wb_ref, *scratch, tq, tk, topk):
    qi = pl.program_id(0)
    h = pl.program_id(1)
    q0 = qi * tq
    n_full = (q0 + 1) // tk
    row = lax.broadcasted_iota(jnp.int32, (tq, tk), 0)
    col = lax.broadcasted_iota(jnp.int32, (tq, tk), 1)
    rc = row - col
    nsub = tk // LANES
    rblk = 128

    def keys(kc):
        return pl.ds(pl.multiple_of(kc * tk, tk), tk)

    def fold(a, op):
        r = a[:, :LANES]
        for t in range(1, nsub):
            r = op(r, a[:, t * LANES:(t + 1) * LANES])
        return r

    @pl.when(h == 0)
    def _select():
        lane = lax.broadcasted_iota(jnp.int32, (tq, LANES), 1)
        for m in range(N_IDX_HEADS // 2):
            qp = qi_ref[:, m * LANES:(m + 1) * LANES].astype(F32)
            qz_ref[2 * m] = jnp.where(lane < IDX_DIM, qp, 0.0).astype(qz_ref.dtype)
            qz_ref[2 * m + 1] = jnp.where(lane >= IDX_DIM, qp, 0.0).astype(qz_ref.dtype)
        wi = wi_ref[...].astype(F32) * (IDX_DIM ** -0.5 * N_IDX_HEADS ** -0.5)
        for j in range(N_IDX_HEADS):
            wb_ref[j] = jnp.broadcast_to(wi[:, j:j + 1], (tq, LANES))

        def chunk_scores(kc):
            kk = kk_ref[keys(kc), :]
            acc = jnp.zeros((tq, tk), F32)
            for j in range(N_IDX_HEADS):
                acc = acc + jnp.maximum(_nt_dot(qz_ref[j], kk), 0.0) * _tile_lanes(wb_ref[j], nsub)
            return acc

        def full_body(kc, carry):
            rmin, rmax = carry
            a = chunk_scores(kc)
            sc_ref[kc] = a
            return jnp.minimum(rmin, fold(a, jnp.minimum)), jnp.maximum(rmax, fold(a, jnp.maximum))

        rmin, rmax = lax.fori_loop(
            0, n_full, full_body,
            (jnp.full((tq, LANES), F32_BIG, F32), jnp.full((tq, LANES), -F32_BIG, F32)))
        a = chunk_scores(n_full)
        causal = rc + (q0 - n_full * tk) >= 0
        sc_ref[n_full] = jnp.where(causal, a, -F32_BIG)
        rmin = jnp.minimum(rmin, fold(jnp.where(causal, a, F32_BIG), jnp.minimum))
        rmax = jnp.maximum(rmax, fold(jnp.where(causal, a, -F32_BIG), jnp.maximum))
        lo0 = jnp.min(rmin, axis=1, keepdims=True)
        hi0 = jnp.max(rmax, axis=1, keepdims=True)

        sc_ref[n_full + 1] = jnp.full((tq, tk), -F32_BIG, F32)
        lane_ones = jnp.ones((LANES, LANES), CDT)

        def count(tb, strict=False):
            cnts = []
            for r in range(tq // rblk):
                groups = [slice(r * rblk + g * 8, r * rblk + (g + 1) * 8) for g in range(rblk // 8)]
                tts = [tb[g] for g in groups]

                def cbody(j, cnt, groups=groups, tts=tts):
                    parts = []
                    for i, (g, t) in enumerate(zip(groups, tts)):
                        c = cnt[i * 8:(i + 1) * 8]
                        for u in range(2):
                            x = sc_ref[2 * j + u, g, :]
                            for l in range(nsub):
                                xl = x[:, l * LANES:(l + 1) * LANES]
                                c = c + jnp.where(xl > t if strict else xl >= t, 1.0, 0.0)
                        parts.append(c)
                    return jnp.concatenate(parts, axis=0)

                cnts.append(lax.fori_loop(0, (n_full + 2) // 2, cbody, jnp.zeros((rblk, LANES), F32)))
            cnt = jnp.concatenate(cnts, axis=0).astype(CDT)
            return jnp.dot(cnt, lane_ones, preferred_element_type=F32)

        kf = float(topk)
        n_causal = (q0 + row[:, :LANES] + 1).astype(F32)
        lo0 = jnp.broadcast_to(lo0, (tq, LANES))
        hi0 = jnp.broadcast_to(hi0 + (jnp.abs(hi0) * 1e-6 + 1e-30), (tq, LANES))

        def cond(c):
            return jnp.logical_and(c[0] < BISECT_MAX_ITERS, c[4])

        def body(c):
            it, lo, hi, cnt_lo, _ = c
            mid = lo + 0.5 * (hi - lo)
            cm = count(mid)
            ge = cm >= kf
            moving = jnp.logical_and(mid > lo, mid < hi)
            lo = jnp.where(ge, mid, lo)
            hi = jnp.where(ge, hi, mid)
            cnt_lo = jnp.where(ge, cm, cnt_lo)
            active = jnp.logical_and(cnt_lo > kf, moving)
            flag = jnp.max(jnp.where(active, 1.0, 0.0)) > 0.0
            return it + 1, lo, hi, cnt_lo, flag

        c0 = (jnp.int32(0), lo0, hi0, n_causal, jnp.max(n_causal) > kf)
        _, lo, _, cnt_lo, _ = lax.while_loop(cond, body, c0)
        lob = _tile_lanes(lo, nsub)
        tied = jnp.max(cnt_lo) > kf

        @pl.when(jnp.logical_not(tied))
        def _():
            def mbody(kc, _):
                sc_ref[kc] = jnp.where(sc_ref[kc] >= lob, 0.0, NEG_INF)
                return 0

            lax.fori_loop(0, n_full + 1, mbody, 0)

        @pl.when(tied)
        def _():
            need = _tile_lanes(kf - count(lo, strict=True), nsub)
            ki = lax.broadcasted_iota(jnp.int32, (tk, tk), 0)
            kj = lax.broadcasted_iota(jnp.int32, (tk, tk), 1)
            prefix_ones = jnp.where(ki <= kj, 1.0, 0.0).astype(CDT)
            chunk_ones = jnp.ones((tk, LANES), CDT)

            def tbody(kc, seen):
                sc = sc_ref[kc]
                eq = sc == lob
                eqf = jnp.where(eq, 1.0, 0.0).astype(CDT)
                rank = jnp.dot(eqf, prefix_ones, preferred_element_type=F32) + _tile_lanes(seen, nsub)
                keep = jnp.logical_or(sc > lob, jnp.logical_and(eq, rank <= need))
                sc_ref[kc] = jnp.where(keep, 0.0, NEG_INF)
                return seen + jnp.dot(eqf, chunk_ones, preferred_element_type=F32)

            lax.fori_loop(0, n_full + 1, tbody, jnp.zeros((tq, LANES), F32))

    cs = slopes_ref[h] * LOG2E
    q = (qa_ref[...].astype(F32) * (HEAD_DIM_A ** -0.5 * LOG2E)).astype(qa_ref.dtype)
    colbias = cs * lax.broadcasted_iota(jnp.int32, (1, tk), 1).astype(F32)
    ones = jnp.ones((tk, LANES), va_ref.dtype)

    def qk(kc):
        return _nt_dot(q, ka_ref[keys(kc), :]) + colbias + sc_ref[kc]

    def v1(kc):
        return jnp.concatenate([va_ref[keys(kc), :], ones], axis=1)

    def shift(kc):
        return cs * (q0 - kc * tk).astype(F32)

    _flash_chunks(n_full, qk, v1, shift, None, *scratch)
    acc = scratch[-1][...]
    y = acc[:, :LANES] / acc[:, LANES:]
    o_ref[...] = (y * _silu(za_ref[...].astype(F32))).astype(o_ref.dtype)


def _dsa(p, slopes, tq, tk, topk):
    s = p.shape[0]
    nq = s // tq
    assert tk >= tq and s % tk == 0
    cb = lambda c: c // LANES
    kern = functools.partial(_dsa_kernel, tq=tq, tk=tk, topk=topk)
    return pl.pallas_call(
        kern,
        out_shape=jax.ShapeDtypeStruct((s, WIDTH), CDT),
        grid=(nq, N_HEADS_A),
        in_specs=[
            pl.BlockSpec(memory_space=pltpu.SMEM),
            pl.BlockSpec((tq, LANES), lambda i, h: (i, cb(C_QA) + h)),
            pl.BlockSpec((s, LANES), lambda i, h: (0, cb(C_KA) + h)),
            pl.BlockSpec((s, LANES), lambda i, h: (0, cb(C_VA) + h)),
            pl.BlockSpec((tq, LANES), lambda i, h: (i, cb(C_ZA) + h)),
            pl.BlockSpec((tq, WIDTH), lambda i, h: (i, C_QI // WIDTH)),
            pl.BlockSpec((s, LANES), lambda i, h: (0, cb(C_KK))),
            pl.BlockSpec((tq, LANES), lambda i, h: (i, cb(C_WI))),
        ],
        out_specs=pl.BlockSpec((tq, LANES), lambda i, h: (i, h)),
        scratch_shapes=[
            pltpu.VMEM((s // tk + 1, tq, tk), F32),
            pltpu.VMEM((N_IDX_HEADS, tq, LANES), CDT),
            pltpu.VMEM((N_IDX_HEADS, tq, LANES), F32),
        ] + _flash_scratch(tq, tk),
        compiler_params=pltpu.CompilerParams(
            dimension_semantics=("parallel", "arbitrary"), vmem_limit_bytes=VMEM_LIMIT),
        name="dsa",
    )(slopes, p, p, p, p, p, p, p)


def _diff_kernel(slopes_ref, lam_ref, qb_ref, kb_ref, vb_ref, zb_ref, g_ref, o_ref, *scratch, tq, tk):
    h = pl.program_id(0)
    qi = pl.program_id(1)
    q0 = qi * tq
    cs = slopes_ref[h] * LOG2E
    lane = lax.broadcasted_iota(jnp.int32, (tq, LANES), 1)
    qf = qb_ref[...].astype(F32) * (DIFF_DIM ** -0.5 * LOG2E)
    qs = jnp.concatenate([jnp.where(lane < DIFF_DIM, qf, 0.0),
                          jnp.where(lane >= DIFF_DIM, qf, 0.0)], axis=0).astype(qb_ref.dtype)
    colbias = cs * lax.broadcasted_iota(jnp.int32, (1, tk), 1).astype(F32)
    ones = jnp.ones((tk, LANES), vb_ref.dtype)

    def keys(kc):
        return pl.ds(pl.multiple_of(kc * tk, tk), tk)

    def qk(kc):
        return _nt_dot(qs, kb_ref[keys(kc), :]) + colbias

    def v1(kc):
        return jnp.concatenate([vb_ref[keys(kc), :], ones], axis=1)

    def shift(kc):
        return cs * (q0 - kc * tk).astype(F32)

    n_full = (q0 + 1) // tk

    def valid(rows):
        r = lax.broadcasted_iota(jnp.int32, (SOFT_ROWS, tk), 0) + rows.start % tq
        c = lax.broadcasted_iota(jnp.int32, (SOFT_ROWS, tk), 1)
        return r - c + (q0 - n_full * tk) >= 0

    _flash_chunks(n_full, qk, v1, shift, valid, *scratch)
    acc = scratch[-1][...]

    lv = lam_ref[...]
    lam = (jnp.exp(jnp.sum(lv[0:1] * lv[1:2], axis=1, keepdims=True))
           - jnp.exp(jnp.sum(lv[2:3] * lv[3:4], axis=1, keepdims=True)) + LAM_INIT)
    y = (acc[:tq, :LANES] / acc[:tq, LANES:]) - lam * (acc[tq:, :LANES] / acc[tq:, LANES:])
    ms = jnp.mean(y * y, axis=-1, keepdims=True)
    y = (y * lax.rsqrt(ms + RMS_EPS) * g_ref[...]) * (1.0 - LAM_INIT)
    o_ref[...] = (y * _silu(zb_ref[...].astype(F32))).astype(o_ref.dtype)


def _diff(p, slopes, lamv, g_subln, tq, tk):
    s = p.shape[0]
    nq = s // tq
    assert tk >= tq and s % tk == 0
    cb = lambda c: c // LANES
    kern = functools.partial(_diff_kernel, tq=tq, tk=tk)
    return pl.pallas_call(
        kern,
        out_shape=jax.ShapeDtypeStruct((s, WIDTH), CDT),
        grid=(N_HEADS_B, nq),
        in_specs=[
            pl.BlockSpec(memory_space=pltpu.SMEM),
            pl.BlockSpec((4, DIFF_DIM), lambda h, i: (0, 0)),
            pl.BlockSpec((tq, LANES), lambda h, i: (i, cb(C_QB) + h)),
            pl.BlockSpec((s, LANES), lambda h, i: (0, cb(C_KB) + h)),
            pl.BlockSpec((s, LANES), lambda h, i: (0, cb(C_VB) + h)),
            pl.BlockSpec((tq, LANES), lambda h, i: (i, cb(C_ZB) + h)),
            pl.BlockSpec((1, LANES), lambda h, i: (0, 0)),
        ],
        out_specs=pl.BlockSpec((tq, LANES), lambda h, i: (i, h)),
        scratch_shapes=_flash_scratch(2 * tq, tk),
        compiler_params=pltpu.CompilerParams(
            dimension_semantics=("parallel", "parallel"), vmem_limit_bytes=VMEM_LIMIT),
        name="diff",
    )(slopes, lamv, p, p, p, p, g_subln)


def _memkv_kernel(mem_ref, g_ref, w_ref, o_ref, h_ref):
    @pl.when(pl.program_id(0) == 0)
    def _():
        xf = mem_ref[...]
        ms = jnp.mean(xf * xf, axis=-1, keepdims=True)
        h_ref[...] = (xf * lax.rsqrt(ms + RMS_EPS) * g_ref[...]).astype(h_ref.dtype)

    o_ref[...] = jnp.dot(h_ref[...], w_ref[...].astype(h_ref.dtype),
                         preferred_element_type=F32).astype(o_ref.dtype)


def _memkv(mem2, g, w, tn):
    nm, d = mem2.shape
    n = w.shape[1]
    return pl.pallas_call(
        _memkv_kernel,
        out_shape=jax.ShapeDtypeStruct((nm, n), CDT),
        grid=(n // tn,),
        in_specs=[
            pl.BlockSpec((nm, d), lambda j: (0, 0)),
            pl.BlockSpec((1, d), lambda j: (0, 0)),
            pl.BlockSpec((d, tn), lambda j: (0, j)),
        ],
        out_specs=pl.BlockSpec((nm, tn), lambda j: (0, j)),
        scratch_shapes=[pltpu.VMEM((nm, d), CDT)],
        compiler_params=pltpu.CompilerParams(
            dimension_semantics=("arbitrary",), vmem_limit_bytes=VMEM_LIMIT),
        name="memkv",
    )(mem2, g, w)


def _memattn_kernel(q_ref, z_ref, k_ref, v_ref, o_ref):
    s = _nt_dot(q_ref[...], k_ref[...]) * (MEM_HEAD_DIM ** -0.5)
    m = jnp.max(s, axis=1, keepdims=True)
    p = jnp.exp(s - m)
    l = jnp.sum(p, axis=1, keepdims=True)
    y = jnp.dot(p.astype(v_ref.dtype), v_ref[...], preferred_element_type=F32) / l
    o_ref[...] = (y * _silu(z_ref[...].astype(F32))).astype(o_ref.dtype)


def _memattn(p, mkv, tm):
    s = p.shape[0]
    nm = mkv.shape[0]
    hd = MEM_HEAD_DIM
    return pl.pallas_call(
        _memattn_kernel,
        out_shape=jax.ShapeDtypeStruct((s, WIDTH), CDT),
        grid=(s // tm, N_MEM_HEADS),
        in_specs=[
            pl.BlockSpec((tm, hd), lambda i, h: (i, C_QM // hd + h)),
            pl.BlockSpec((tm, hd), lambda i, h: (i, C_ZM // hd + h)),
            pl.BlockSpec((nm, hd), lambda i, h: (0, h)),
            pl.BlockSpec((nm, hd), lambda i, h: (0, N_MEM_HEADS + h)),
        ],
        out_specs=pl.BlockSpec((tm, hd), lambda i, h: (i, h)),
        compiler_params=pltpu.CompilerParams(
            dimension_semantics=("parallel", "parallel"), vmem_limit_bytes=VMEM_LIMIT),
        name="memattn",
    )(p, p, mkv, mkv)


def _merge_kernel(ua_ref, ub_ref, um_ref, ga_ref, gb_ref, gm_ref, b_ref, wa_ref, wb_ref, wm_ref, o_ref):
    d = o_ref.shape[1]
    acc = None
    for t, (u_ref, g_ref, w_ref) in enumerate(
            ((ua_ref, ga_ref, wa_ref), (ub_ref, gb_ref, wb_ref), (um_ref, gm_ref, wm_ref))):
        gate = jax.nn.sigmoid(g_ref[...].astype(F32) + b_ref[:, t * d:(t + 1) * d])
        term = gate * jnp.dot(u_ref[...], w_ref[...], preferred_element_type=F32)
        acc = term if acc is None else acc + term
    o_ref[...] = acc.astype(o_ref.dtype)


def _merge(ua, ub, um, p, b_gate, wa, wb, wm, tm):
    s = ua.shape[0]
    d = D_MODEL
    once = pl.Buffered(1)
    u_spec = pl.BlockSpec((tm, WIDTH), lambda i: (i, 0))
    w_spec = pl.BlockSpec((WIDTH, d), lambda i: (0, 0), pipeline_mode=once)
    return pl.pallas_call(
        _merge_kernel,
        out_shape=jax.ShapeDtypeStruct((s, d), CDT),
        grid=(s // tm,),
        in_specs=[
            u_spec, u_spec, u_spec,
            pl.BlockSpec((tm, d), lambda i: (i, C_GL // d + 0)),
            pl.BlockSpec((tm, d), lambda i: (i, C_GL // d + 1)),
            pl.BlockSpec((tm, d), lambda i: (i, C_GL // d + 2)),
            pl.BlockSpec((1, N_BRANCH * d), lambda i: (0, 0)),
            w_spec, w_spec, w_spec,
        ],
        out_specs=pl.BlockSpec((tm, d), lambda i: (i, 0)),
        compiler_params=pltpu.CompilerParams(
            dimension_semantics=("parallel",), vmem_limit_bytes=VMEM_LIMIT),
        name="merge",
    )(ua, ub, um, p, p, p, b_gate, wa, wb, wm)


def _out_kernel(x_ref, mg_ref, w_ref, g_ref, o_ref):
    y = x_ref[...] + jnp.dot(mg_ref[...], w_ref[...], preferred_element_type=F32)
    ms = jnp.mean(y * y, axis=-1, keepdims=True)
    o_ref[...] = (y * lax.rsqrt(ms + RMS_EPS) * g_ref[...]).astype(o_ref.dtype)


def _out(x2, merged, w_out, g_final, tm):
    s, d = x2.shape
    return pl.pallas_call(
        _out_kernel,
        out_shape=jax.ShapeDtypeStruct((s, d), x2.dtype),
        grid=(s // tm,),
        in_specs=[
            pl.BlockSpec((tm, d), lambda i: (i, 0)),
            pl.BlockSpec((tm, d), lambda i: (i, 0)),
            pl.BlockSpec((d, d), lambda i: (0, 0), pipeline_mode=pl.Buffered(1)),
            pl.BlockSpec((1, d), lambda i: (0, 0)),
        ],
        out_specs=pl.BlockSpec((tm, d), lambda i: (i, 0)),
        compiler_params=pltpu.CompilerParams(
            dimension_semantics=("parallel",), vmem_limit_bytes=VMEM_LIMIT),
        name="out",
    )(x2, merged, w_out, g_final)


def _column_moves():
    src = {}
    o = 0
    for name, width in (("qa", WIDTH), ("ka", WIDTH), ("va", WIDTH), ("za", WIDTH),
                        ("qi", N_IDX_HEADS * IDX_DIM), ("ki", IDX_DIM), ("wi", N_IDX_HEADS),
                        ("qb", WIDTH), ("kb", WIDTH), ("vb", WIDTH), ("zb", WIDTH),
                        ("qm", WIDTH), ("zm", WIDTH), ("gl", N_BRANCH * D_MODEL)):
        src[name] = (o, width)
        o += width
    dst = (("gl", C_GL), ("qi", C_QI), ("qa", C_QA), ("ka", C_KA), ("va", C_VA), ("za", C_ZA),
           ("qb", C_QB), ("kb", C_KB), ("vb", C_VB), ("zb", C_ZB), ("qm", C_QM), ("zm", C_ZM),
           ("ki", C_KK), ("ki", C_KK + IDX_DIM), ("wi", C_WI))
    return o, [(d,) + src[name] for name, d in dst]


WPREP_ROWS = 256
WPREP_ALIGN = 16


def _wprep_kernel(off_ref, w_ref, o_ref):
    del off_ref
    last = pl.num_programs(0) - 1

    @pl.when(pl.program_id(0) != last)
    def _():
        o_ref[...] = w_ref[...].astype(o_ref.dtype)

    @pl.when(pl.program_id(0) == last)
    def _():
        ki = w_ref[:IDX_DIM, :].astype(o_ref.dtype)
        o_ref[:IDX_DIM, :] = ki
        o_ref[IDX_DIM:2 * IDX_DIM, :] = ki
        o_ref[2 * IDX_DIM:2 * IDX_DIM + N_IDX_HEADS, :] = (
            w_ref[IDX_DIM:IDX_DIM + N_IDX_HEADS, :].astype(o_ref.dtype))
        o_ref[2 * IDX_DIM + N_IDX_HEADS:, :] = jnp.zeros(
            (WPREP_ROWS - 2 * IDX_DIM - N_IDX_HEADS, o_ref.shape[1]), o_ref.dtype)


def _reorder_w_in_t(w):
    d_in, moves = _column_moves()
    assert w.shape[1] == d_in and D_PAD - C_KK == WPREP_ROWS
    wt = jnp.swapaxes(w, 0, 1)
    offs = []
    for d, s0, width in sorted(moves):
        if d < C_KK:
            assert width % WPREP_ROWS == 0 and d == len(offs) * WPREP_ROWS
            offs += list(range(s0, s0 + width, WPREP_ROWS))
    offs.append(dict((d, s0) for d, s0, _ in moves)[C_KK])
    assert all(o % WPREP_ALIGN == 0 for o in offs)
    n_d = w.shape[0]
    return pl.pallas_call(
        _wprep_kernel,
        out_shape=jax.ShapeDtypeStruct((D_PAD, n_d), CDT),
        grid_spec=pltpu.PrefetchScalarGridSpec(
            num_scalar_prefetch=1,
            grid=(len(offs),),
            in_specs=[pl.BlockSpec((pl.Element(WPREP_ROWS), pl.Element(n_d)),
                                   lambda i, off: (off[i] * WPREP_ALIGN, 0))],
            out_specs=pl.BlockSpec((WPREP_ROWS, n_d), lambda i, off: (i, 0)),
        ),
        compiler_params=pltpu.CompilerParams(
            dimension_semantics=("parallel",), vmem_limit_bytes=VMEM_LIMIT),
        name="wprep",
    )(jnp.asarray([o // WPREP_ALIGN for o in offs], jnp.int32), wt)


def kernel(x, mem, g_in, w_in, b_gate, lam_q1, lam_k1, lam_q2, lam_k2, g_subln, g_mem,
           w_mem_kv, w_br_a, w_br_b, w_br_m, w_out, g_final):
    bsz, s_len, d = x.shape
    assert bsz == 1 and d == D_MODEL and g_in.shape[0] == 1
    topk = min(TOPK_MAX, s_len // 4)
    x2 = x.reshape(s_len, d)
    slopes = 2.0 ** (-8.0 * jnp.arange(1, N_HEADS_A + 1, dtype=F32) / N_HEADS_A)

    p = _proj(x2, g_in[0].reshape(1, d), _reorder_w_in_t(w_in[0]), tm=1024, tn=768)
    ua = _dsa(p, slopes, tq=512, tk=512, topk=topk)
    lamv = jnp.stack([lam_q1[0], lam_k1[0], lam_q2[0], lam_k2[0]]).astype(F32)
    ub = _diff(p, slopes, lamv, g_subln[0].reshape(1, 2 * DIFF_DIM), tq=512, tk=512)
    mkv = _memkv(mem.reshape(-1, d), g_mem[0].reshape(1, d), w_mem_kv[0], tn=512)
    um = _memattn(p, mkv, tm=1024)
    merged = _merge(ua, ub, um, p, b_gate[0].reshape(1, -1),
                    w_br_a[0].astype(CDT), w_br_b[0].astype(CDT), w_br_m[0].astype(CDT), tm=512)
    y = _out(x2, merged, w_out[0].astype(CDT), g_final.reshape(1, d), tm=512)
    return y.reshape(bsz, s_len, d)
```

```python
import functools

import jax
import jax.numpy as jnp
from jax import lax
from jax.experimental import pallas as pl
from jax.experimental.pallas import tpu as pltpu

F32 = jnp.float32
CDT = jnp.bfloat16

D_MODEL = 2048
N_HEADS_A = 8
HEAD_DIM_A = 128
N_IDX_HEADS = 16
IDX_DIM = 64
TOPK_MAX = 256
N_HEADS_B = 8
DIFF_DIM = 64
N_MEM_HEADS = 4
MEM_HEAD_DIM = 256
N_BRANCH = 3
WIDTH = 1024
RMS_EPS = 1e-6
NEG_INF = -1e30
F32_BIG = 3.0e38
LAM_INIT = 0.2
LANES = 128
LOG2E = 1.4426950408889634

C_GL = 0
C_QI = 6144
C_QA = 7168
C_KA = 8192
C_VA = 9216
C_ZA = 10240
C_QB = 11264
C_KB = 12288
C_VB = 13312
C_ZB = 14336
C_QM = 15360
C_ZM = 16384
C_KK = 17408
C_WI = 17536
D_PAD = 17664

VMEM_LIMIT = 56 * 1024 * 1024

BISECT_MAX_ITERS = 96
FLASH_PAIRS_PER_TRIP = 2
NORM_ROWS = 128
SOFT_ROWS = 16


def _nt_dot(a, b):
    return lax.dot_general(a, b, (((1,), (1,)), ((), ())), preferred_element_type=F32)


def _silu(z):
    return z * jax.nn.sigmoid(z)


def _proj_kernel(x_ref, g_ref, w_ref, o_ref, h_ref):
    @pl.when(pl.program_id(1) == 0)
    def _():
        for r in range(0, x_ref.shape[0], NORM_ROWS):
            xf = x_ref[r:r + NORM_ROWS, :]
            ms = jnp.mean(xf * xf, axis=-1, keepdims=True)
            h_ref[r:r + NORM_ROWS, :] = (xf * lax.rsqrt(ms + RMS_EPS) * g_ref[...]).astype(h_ref.dtype)

    o_ref[...] = _nt_dot(h_ref[...], w_ref[...]).astype(o_ref.dtype)


def _proj(x2, g, wt, tm, tn):
    s, d = x2.shape
    n = wt.shape[0]
    return pl.pallas_call(
        _proj_kernel,
        out_shape=jax.ShapeDtypeStruct((s, n), CDT),
        grid=(s // tm, n // tn),
        in_specs=[
            pl.BlockSpec((tm, d), lambda i, j: (i, 0)),
            pl.BlockSpec((1, d), lambda i, j: (0, 0)),
            pl.BlockSpec((tn, d), lambda i, j: (j, 0)),
        ],
        out_specs=pl.BlockSpec((tm, tn), lambda i, j: (i, j)),
        scratch_shapes=[pltpu.VMEM((tm, d), CDT)],
        compiler_params=pltpu.CompilerParams(
            dimension_semantics=("parallel", "arbitrary"), vmem_limit_bytes=VMEM_LIMIT),
        name="proj",
    )(x2, g, wt)


def _tile_lanes(x, n):
    return x if n == 1 else jnp.concatenate([x] * n, axis=1)


def _flash_scratch(rows, tk):
    return [pltpu.VMEM((rows, tk), F32), pltpu.VMEM((rows, tk), F32),
            pltpu.VMEM((rows, tk), CDT), pltpu.VMEM((rows, tk), CDT),
            pltpu.VMEM((rows, LANES), F32), pltpu.VMEM((rows, LANES), F32),
            pltpu.VMEM((rows, LANES), F32), pltpu.VMEM((rows, 2 * LANES), F32)]


def _flash_chunks(n_last, qk, v1, shift, valid, masked_pair, s_a, s_b, p_a, p_b, al_a, al_b, m_ref,
                  acc_ref):
    tk = s_a.shape[1]

    def soft(s_ref, p_ref, al_ref, c, masked=False):
        d = shift(c)
        for g in range(s_ref.shape[0] // SOFT_ROWS):
            rows = slice(g * SOFT_ROWS, (g + 1) * SOFT_ROWS)
            s = s_ref[rows, :]
            if masked and valid is not None:
                s = jnp.where(valid(rows, c), s, NEG_INF)
            m = m_ref[rows, :]
            m_new = jnp.maximum(m, jnp.max(s, axis=1, keepdims=True) - d)
            al_ref[rows, :] = jnp.exp2(m - m_new)
            m_ref[rows, :] = m_new
            p_ref[rows, :] = jnp.exp2(s - _tile_lanes(m_new + d, tk // LANES)).astype(p_ref.dtype)

    def pv(p_ref, al_ref, c):
        acc_ref[...] = (_tile_lanes(al_ref[...], 2) * acc_ref[...]
                        + jnp.dot(p_ref[...], v1(c), preferred_element_type=F32))

    m_ref[...] = jnp.full(m_ref.shape, NEG_INF, F32)
    acc_ref[...] = jnp.zeros(acc_ref.shape, F32)
    s_a[...] = qk(0)
    p_b[...] = jnp.zeros(p_b.shape, p_b.dtype)
    al_b[...] = jnp.ones(al_b.shape, F32)

    def pair(c0):
        pv(p_b, al_b, jnp.maximum(c0 - 1, 0))
        s_b[...] = qk(c0 + 1)
        soft(s_a, p_a, al_a, c0)
        pv(p_a, al_a, c0)
        s_a[...] = qk(c0 + 2)
        soft(s_b, p_b, al_b, c0 + 1)

    span = 2 * FLASH_PAIRS_PER_TRIP

    def multi_body(j, _):
        for u in range(FLASH_PAIRS_PER_TRIP):
            pair(span * j + 2 * u)
        return 0

    def pair_body(j, _):
        pair(span * (n_last // span) + 2 * j)
        return 0

    lax.fori_loop(0, n_last // span, multi_body, 0)
    lax.fori_loop(0, (n_last % span) // 2, pair_body, 0)

    def odd_tail():
        c0 = n_last - 1
        pv(p_b, al_b, jnp.maximum(c0 - 1, 0))
        s_b[...] = qk(n_last)
        soft(s_a, p_a, al_a, c0, masked=masked_pair)
        pv(p_a, al_a, c0)
        soft(s_b, p_b, al_b, n_last, masked=True)
        pv(p_b, al_b, n_last)

    def even_tail():
        pv(p_b, al_b, jnp.maximum(n_last - 1, 0))
        soft(s_a, p_a, al_a, n_last, masked=True)
        pv(p_a, al_a, n_last)

    if masked_pair:
        odd_tail()
    else:
        pl.when(n_last % 2 == 1)(odd_tail)
        pl.when(n_last % 2 == 0)(even_tail)


def _dsa_kernel(slopes_ref, qa_ref, ka_ref, va_ref, za_ref, qi_ref, kk_ref, wi_ref, o_ref,
                sc_ref, qz_ref, wb_ref, tb_ref, cnt_ref, *scratch, tq, tk, topk):
    qi = pl.program_id(0)
    h = pl.program_id(1)
    q0 = qi * tq
    n_full = (q0 + 1) // tk
    row = lax.broadcasted_iota(jnp.int32, (tq, tk), 0)
    col = lax.broadcasted_iota(jnp.int32, (tq, tk), 1)
    rc = row - col
    nsub = tk // LANES
    rblk = LANES

    def keys(kc):
        return pl.ds(pl.multiple_of(kc * tk, tk), tk)

    def fold(a, op):
        r = a[:, :LANES]
        for t in range(1, nsub):
            r = op(r, a[:, t * LANES:(t + 1) * LANES])
        return r

    @pl.when(h == 0)
    def _select():
        lane = lax.broadcasted_iota(jnp.int32, (tq, LANES), 1)
        for m in range(N_IDX_HEADS // 2):
            qp = qi_ref[:, m * LANES:(m + 1) * LANES].astype(F32)
            qz_ref[2 * m] = jnp.where(lane < IDX_DIM, qp, 0.0).astype(qz_ref.dtype)
            qz_ref[2 * m + 1] = jnp.where(lane >= IDX_DIM, qp, 0.0).astype(qz_ref.dtype)
        wi = wi_ref[...].astype(F32) * (IDX_DIM ** -0.5 * N_IDX_HEADS ** -0.5)
        for j in range(N_IDX_HEADS):
            wb_ref[j] = jnp.broadcast_to(wi[:, j:j + 1], (tq, LANES))

        def chunk_scores(kc):
            kk = kk_ref[keys(kc), :]
            acc = jnp.zeros((tq, tk), F32)
            for j in range(N_IDX_HEADS):
                acc = acc + jnp.maximum(_nt_dot(qz_ref[j], kk), 0.0) * _tile_lanes(wb_ref[j], nsub)
            return acc

        def full_body(kc, carry):
            rmin, rmax = carry
            a = chunk_scores(kc)
            sc_ref[kc] = a
            return jnp.minimum(rmin, fold(a, jnp.minimum)), jnp.maximum(rmax, fold(a, jnp.maximum))

        rmin, rmax = lax.fori_loop(
            0, n_full, full_body,
            (jnp.full((tq, LANES), F32_BIG, F32), jnp.full((tq, LANES), -F32_BIG, F32)))
        a = chunk_scores(n_full)
        causal = rc + (q0 - n_full * tk) >= 0
        sc_ref[n_full] = jnp.where(causal, a, -F32_BIG)
        rmin = jnp.minimum(rmin, fold(jnp.where(causal, a, F32_BIG), jnp.minimum))
        rmax = jnp.maximum(rmax, fold(jnp.where(causal, a, -F32_BIG), jnp.maximum))
        nblk = tq // rblk
        sub = lax.broadcasted_iota(jnp.int32, (8, LANES), 0)
        lane_id = lax.broadcasted_iota(jnp.int32, (8, LANES), 1)
        pad = [jnp.zeros((8 - nblk, LANES), F32)]

        def compact(x, reduce):
            return jnp.concatenate(
                [reduce(x[r * rblk:(r + 1) * rblk].T, axis=0, keepdims=True) for r in range(nblk)] + pad,
                axis=0)

        def expand(c):
            return jnp.concatenate(
                [jnp.broadcast_to(c[r:r + 1, :], (rblk, LANES)).T for r in range(nblk)], axis=0)

        sc_ref[n_full + 1] = jnp.full((tq, tk), -F32_BIG, F32)

        def count(tc, strict=False):
            tb_ref[...] = expand(tc)
            cnt_ref[...] = jnp.zeros(cnt_ref.shape, F32)

            def cbody(j, _):
                for g in range(tq // 8):
                    rows = slice(g * 8, (g + 1) * 8)
                    t = tb_ref[rows, :]
                    c = cnt_ref[rows, :]
                    for u in range(2):
                        x = sc_ref[2 * j + u, rows, :]
                        for l in range(nsub):
                            xl = x[:, l * LANES:(l + 1) * LANES]
                            c = c + jnp.where(xl > t if strict else xl >= t, 1.0, 0.0)
                    cnt_ref[rows, :] = c
                return 0

            lax.fori_loop(0, (n_full + 2) // 2, cbody, 0)
            return compact(cnt_ref[...], jnp.sum)

        kf = float(topk)
        n_causal = jnp.where(sub < nblk, q0 + sub * rblk + lane_id + 1, 0).astype(F32)
        lo0 = compact(rmin, jnp.min)
        hi0 = compact(rmax, jnp.max)
        hi0 = hi0 + (jnp.abs(hi0) * 1e-6 + 1e-30)

        def cond(c):
            return jnp.logical_and(c[0] < BISECT_MAX_ITERS, c[4])

        def body(c):
            it, lo, hi, cnt_lo, _ = c
            mid = lo + 0.5 * (hi - lo)
            cm = count(mid)
            ge = cm >= kf
            moving = jnp.logical_and(mid > lo, mid < hi)
            lo = jnp.where(ge, mid, lo)
            hi = jnp.where(ge, hi, mid)
            cnt_lo = jnp.where(ge, cm, cnt_lo)
            active = jnp.logical_and(cnt_lo > kf, moving)
            flag = jnp.max(jnp.where(active, 1.0, 0.0)) > 0.0
            return it + 1, lo, hi, cnt_lo, flag

        c0 = (jnp.int32(0), lo0, hi0, n_causal, jnp.max(n_causal) > kf)
        _, lo, _, cnt_lo, _ = lax.while_loop(cond, body, c0)
        lob = _tile_lanes(expand(lo), nsub)
        tied = jnp.max(cnt_lo) > kf

        @pl.when(jnp.logical_not(tied))
        def _():
            def mbody(kc, _):
                sc_ref[kc] = jnp.where(sc_ref[kc] >= lob, 0.0, NEG_INF)
                return 0

            lax.fori_loop(0, n_full + 1, mbody, 0)

        @pl.when(tied)
        def _():
            need = _tile_lanes(expand(kf - count(lo, strict=True)), nsub)
            ki = lax.broadcasted_iota(jnp.int32, (tk, tk), 0)
            kj = lax.broadcasted_iota(jnp.int32, (tk, tk), 1)
            prefix_ones = jnp.where(ki <= kj, 1.0, 0.0).astype(CDT)
            chunk_ones = jnp.ones((tk, LANES), CDT)

            def tbody(kc, seen):
                sc = sc_ref[kc]
                eq = sc == lob
                eqf = jnp.where(eq, 1.0, 0.0).astype(CDT)
                rank = jnp.dot(eqf, prefix_ones, preferred_element_type=F32) + _tile_lanes(seen, nsub)
                keep = jnp.logical_or(sc > lob, jnp.logical_and(eq, rank <= need))
                sc_ref[kc] = jnp.where(keep, 0.0, NEG_INF)
                return seen + jnp.dot(eqf, chunk_ones, preferred_element_type=F32)

            lax.fori_loop(0, n_full + 1, tbody, jnp.zeros((tq, LANES), F32))

    cs = slopes_ref[h] * LOG2E
    q = (qa_ref[...].astype(F32) * (HEAD_DIM_A ** -0.5 * LOG2E)).astype(qa_ref.dtype)
    colbias = cs * lax.broadcasted_iota(jnp.int32, (1, tk), 1).astype(F32)
    ones = jnp.ones((tk, LANES), va_ref.dtype)

    def qk(kc):
        return _nt_dot(q, ka_ref[keys(kc), :]) + colbias + sc_ref[kc]

    def v1(kc):
        return jnp.concatenate([va_ref[keys(kc), :], ones], axis=1)

    def shift(kc):
        return cs * (q0 - kc * tk).astype(F32)

    _flash_chunks(n_full, qk, v1, shift, None, False, *scratch)
    acc = scratch[-1][...]
    y = acc[:, :LANES] / acc[:, LANES:]
    o_ref[...] = (y * _silu(za_ref[...].astype(F32))).astype(o_ref.dtype)


def _dsa(p, slopes, tq, tk, topk):
    s = p.shape[0]
    nq = s // tq
    assert tk >= tq and s % tk == 0
    cb = lambda c: c // LANES
    kern = functools.partial(_dsa_kernel, tq=tq, tk=tk, topk=topk)
    return pl.pallas_call(
        kern,
        out_shape=jax.ShapeDtypeStruct((s, WIDTH), CDT),
        grid=(nq, N_HEADS_A),
        in_specs=[
            pl.BlockSpec(memory_space=pltpu.SMEM),
            pl.BlockSpec((tq, LANES), lambda i, h: (i, cb(C_QA) + h)),
            pl.BlockSpec((s, LANES), lambda i, h: (0, cb(C_KA) + h)),
            pl.BlockSpec((s, LANES), lambda i, h: (0, cb(C_VA) + h)),
            pl.BlockSpec((tq, LANES), lambda i, h: (i, cb(C_ZA) + h)),
            pl.BlockSpec((tq, WIDTH), lambda i, h: (i, C_QI // WIDTH)),
            pl.BlockSpec((s, LANES), lambda i, h: (0, cb(C_KK))),
            pl.BlockSpec((tq, LANES), lambda i, h: (i, cb(C_WI))),
        ],
        out_specs=pl.BlockSpec((tq, LANES), lambda i, h: (i, h)),
        scratch_shapes=[
            pltpu.VMEM((s // tk + 1, tq, tk), F32),
            pltpu.VMEM((N_IDX_HEADS, tq, LANES), CDT),
            pltpu.VMEM((N_IDX_HEADS, tq, LANES), F32),
            pltpu.VMEM((tq, LANES), F32),
            pltpu.VMEM((tq, LANES), F32),
        ] + _flash_scratch(tq, tk),
        compiler_params=pltpu.CompilerParams(
            dimension_semantics=("parallel", "arbitrary"), vmem_limit_bytes=VMEM_LIMIT),
        name="dsa",
    )(slopes, p, p, p, p, p, p, p)


def _diff_kernel(slopes_ref, lam_ref, qb_ref, kb_ref, vb_ref, zb_ref, g_ref, o_ref, *scratch, tq, tk):
    h = pl.program_id(0)
    qi = pl.program_id(1)
    q0 = qi * tq
    cs = slopes_ref[h] * LOG2E
    lane = lax.broadcasted_iota(jnp.int32, (tq, LANES), 1)
    qf = qb_ref[...].astype(F32) * (DIFF_DIM ** -0.5 * LOG2E)
    qs = jnp.concatenate([jnp.where(lane < DIFF_DIM, qf, 0.0),
                          jnp.where(lane >= DIFF_DIM, qf, 0.0)], axis=0).astype(qb_ref.dtype)
    colbias = cs * lax.broadcasted_iota(jnp.int32, (1, tk), 1).astype(F32)
    ones = jnp.ones((tk, LANES), vb_ref.dtype)

    def keys(kc):
        return pl.ds(pl.multiple_of(kc * tk, tk), tk)

    def qk(kc):
        return _nt_dot(qs, kb_ref[keys(kc), :]) + colbias

    def v1(kc):
        return jnp.concatenate([vb_ref[keys(kc), :], ones], axis=1)

    def shift(kc):
        return cs * (q0 - kc * tk).astype(F32)

    n_last = (q0 + tq - 1) // tk

    def valid(rows, c):
        r = lax.broadcasted_iota(jnp.int32, (SOFT_ROWS, tk), 0) + rows.start % tq
        col = lax.broadcasted_iota(jnp.int32, (SOFT_ROWS, tk), 1)
        return r - col + (q0 - c * tk) >= 0

    _flash_chunks(n_last, qk, v1, shift, valid, tq == 2 * tk, *scratch)
    acc = scratch[-1][...]

    lv = lam_ref[...]
    lam = (jnp.exp(jnp.sum(lv[0:1] * lv[1:2], axis=1, keepdims=True))
           - jnp.exp(jnp.sum(lv[2:3] * lv[3:4], axis=1, keepdims=True)) + LAM_INIT)
    y = (acc[:tq, :LANES] / acc[:tq, LANES:]) - lam * (acc[tq:, :LANES] / acc[tq:, LANES:])
    ms = jnp.mean(y * y, axis=-1, keepdims=True)
    y = (y * lax.rsqrt(ms + RMS_EPS) * g_ref[...]) * (1.0 - LAM_INIT)
    o_ref[...] = (y * _silu(zb_ref[...].astype(F32))).astype(o_ref.dtype)


def _diff(p, slopes, lamv, g_subln, tq, tk):
    s = p.shape[0]
    nq = s // tq
    assert tq in (tk, 2 * tk) and s % tq == 0
    cb = lambda c: c // LANES
    kern = functools.partial(_diff_kernel, tq=tq, tk=tk)
    return pl.pallas_call(
        kern,
        out_shape=jax.ShapeDtypeStruct((s, WIDTH), CDT),
        grid=(N_HEADS_B, nq),
        in_specs=[
            pl.BlockSpec(memory_space=pltpu.SMEM),
            pl.BlockSpec((4, DIFF_DIM), lambda h, i: (0, 0)),
            pl.BlockSpec((tq, LANES), lambda h, i: (i, cb(C_QB) + h)),
            pl.BlockSpec((s, LANES), lambda h, i: (0, cb(C_KB) + h)),
            pl.BlockSpec((s, LANES), lambda h, i: (0, cb(C_VB) + h)),
            pl.BlockSpec((tq, LANES), lambda h, i: (i, cb(C_ZB) + h)),
            pl.BlockSpec((1, LANES), lambda h, i: (0, 0)),
        ],
        out_specs=pl.BlockSpec((tq, LANES), lambda h, i: (i, h)),
        scratch_shapes=_flash_scratch(2 * tq, tk),
        compiler_params=pltpu.CompilerParams(
            dimension_semantics=("parallel", "parallel"), vmem_limit_bytes=VMEM_LIMIT),
        name="diff",
    )(slopes, lamv, p, p, p, p, g_subln)


def _memkv_kernel(mem_ref, g_ref, w_ref, o_ref, h_ref):
    @pl.when(pl.program_id(0) == 0)
    def _():
        xf = mem_ref[...]
        ms = jnp.mean(xf * xf, axis=-1, keepdims=True)
        h_ref[...] = (xf * lax.rsqrt(ms + RMS_EPS) * g_ref[...]).astype(h_ref.dtype)

    o_ref[...] = jnp.dot(h_ref[...], w_ref[...].astype(h_ref.dtype),
                         preferred_element_type=F32).astype(o_ref.dtype)


def _memkv(mem2, g, w, tn):
    nm, d = mem2.shape
    n = w.shape[1]
    return pl.pallas_call(
        _memkv_kernel,
        out_shape=jax.ShapeDtypeStruct((nm, n), CDT),
        grid=(n // tn,),
        in_specs=[
            pl.BlockSpec((nm, d), lambda j: (0, 0)),
            pl.BlockSpec((1, d), lambda j: (0, 0)),
            pl.BlockSpec((d, tn), lambda j: (0, j)),
        ],
        out_specs=pl.BlockSpec((nm, tn), lambda j: (0, j)),
        scratch_shapes=[pltpu.VMEM((nm, d), CDT)],
        compiler_params=pltpu.CompilerParams(
            dimension_semantics=("arbitrary",), vmem_limit_bytes=VMEM_LIMIT),
        name="memkv",
    )(mem2, g, w)


def _memattn_kernel(q_ref, z_ref, k_ref, v_ref, o_ref):
    s = _nt_dot(q_ref[...], k_ref[...]) * (MEM_HEAD_DIM ** -0.5)
    m = jnp.max(s, axis=1, keepdims=True)
    p = jnp.exp(s - m)
    l = jnp.sum(p, axis=1, keepdims=True)
    y = jnp.dot(p.astype(v_ref.dtype), v_ref[...], preferred_element_type=F32) / l
    o_ref[...] = (y * _silu(z_ref[...].astype(F32))).astype(o_ref.dtype)


def _memattn(p, mkv, tm):
    s = p.shape[0]
    nm = mkv.shape[0]
    hd = MEM_HEAD_DIM
    return pl.pallas_call(
        _memattn_kernel,
        out_shape=jax.ShapeDtypeStruct((s, WIDTH), CDT),
        grid=(s // tm, N_MEM_HEADS),
        in_specs=[
            pl.BlockSpec((tm, hd), lambda i, h: (i, C_QM // hd + h)),
            pl.BlockSpec((tm, hd), lambda i, h: (i, C_ZM // hd + h)),
            pl.BlockSpec((nm, hd), lambda i, h: (0, h)),
            pl.BlockSpec((nm, hd), lambda i, h: (0, N_MEM_HEADS + h)),
        ],
        out_specs=pl.BlockSpec((tm, hd), lambda i, h: (i, h)),
        compiler_params=pltpu.CompilerParams(
            dimension_semantics=("parallel", "parallel"), vmem_limit_bytes=VMEM_LIMIT),
        name="memattn",
    )(p, p, mkv, mkv)


def _merge_kernel(ua_ref, ub_ref, um_ref, ga_ref, gb_ref, gm_ref, b_ref, wa_ref, wb_ref, wm_ref, o_ref):
    d = o_ref.shape[1]
    acc = None
    for t, (u_ref, g_ref, w_ref) in enumerate(
            ((ua_ref, ga_ref, wa_ref), (ub_ref, gb_ref, wb_ref), (um_ref, gm_ref, wm_ref))):
        gate = jax.nn.sigmoid(g_ref[...].astype(F32) + b_ref[:, t * d:(t + 1) * d])
        term = gate * jnp.dot(u_ref[...], w_ref[...], preferred_element_type=F32)
        acc = term if acc is None else acc + term
    o_ref[...] = acc.astype(o_ref.dtype)


def _merge(ua, ub, um, p, b_gate, wa, wb, wm, tm):
    s = ua.shape[0]
    d = D_MODEL
    once = pl.Buffered(1)
    u_spec = pl.BlockSpec((tm, WIDTH), lambda i: (i, 0))
    w_spec = pl.BlockSpec((WIDTH, d), lambda i: (0, 0), pipeline_mode=once)
    return pl.pallas_call(
        _merge_kernel,
        out_shape=jax.ShapeDtypeStruct((s, d), CDT),
        grid=(s // tm,),
        in_specs=[
            u_spec, u_spec, u_spec,
            pl.BlockSpec((tm, d), lambda i: (i, C_GL // d + 0)),
            pl.BlockSpec((tm, d), lambda i: (i, C_GL // d + 1)),
            pl.BlockSpec((tm, d), lambda i: (i, C_GL // d + 2)),
            pl.BlockSpec((1, N_BRANCH * d), lambda i: (0, 0)),
            w_spec, w_spec, w_spec,
        ],
        out_specs=pl.BlockSpec((tm, d), lambda i: (i, 0)),
        compiler_params=pltpu.CompilerParams(
            dimension_semantics=("parallel",), vmem_limit_bytes=VMEM_LIMIT),
        name="merge",
    )(ua, ub, um, p, p, p, b_gate, wa, wb, wm)


def _out_kernel(x_ref, mg_ref, w_ref, g_ref, o_ref):
    y = x_ref[...] + jnp.dot(mg_ref[...], w_ref[...], preferred_element_type=F32)
    ms = jnp.mean(y * y, axis=-1, keepdims=True)
    o_ref[...] = (y * lax.rsqrt(ms + RMS_EPS) * g_ref[...]).astype(o_ref.dtype)


def _out(x2, merged, w_out, g_final, tm):
    s, d = x2.shape
    return pl.pallas_call(
        _out_kernel,
        out_shape=jax.ShapeDtypeStruct((s, d), x2.dtype),
        grid=(s // tm,),
        in_specs=[
            pl.BlockSpec((tm, d), lambda i: (i, 0)),
            pl.BlockSpec((tm, d), lambda i: (i, 0)),
            pl.BlockSpec((d, d), lambda i: (0, 0), pipeline_mode=pl.Buffered(1)),
            pl.BlockSpec((1, d), lambda i: (0, 0)),
        ],
        out_specs=pl.BlockSpec((tm, d), lambda i: (i, 0)),
        compiler_params=pltpu.CompilerParams(
            dimension_semantics=("parallel",), vmem_limit_bytes=VMEM_LIMIT),
        name="out",
    )(x2, merged, w_out, g_final)


def _column_moves():
    src = {}
    o = 0
    for name, width in (("qa", WIDTH), ("ka", WIDTH), ("va", WIDTH), ("za", WIDTH),
                        ("qi", N_IDX_HEADS * IDX_DIM), ("ki", IDX_DIM), ("wi", N_IDX_HEADS),
                        ("qb", WIDTH), ("kb", WIDTH), ("vb", WIDTH), ("zb", WIDTH),
                        ("qm", WIDTH), ("zm", WIDTH), ("gl", N_BRANCH * D_MODEL)):
        src[name] = (o, width)
        o += width
    dst = (("gl", C_GL), ("qi", C_QI), ("qa", C_QA), ("ka", C_KA), ("va", C_VA), ("za", C_ZA),
           ("qb", C_QB), ("kb", C_KB), ("vb", C_VB), ("zb", C_ZB), ("qm", C_QM), ("zm", C_ZM),
           ("ki", C_KK), ("ki", C_KK + IDX_DIM), ("wi", C_WI))
    return o, [(d,) + src[name] for name, d in dst]


WPREP_ROWS = 256
WPREP_ALIGN = 16


def _wprep_kernel(off_ref, w_ref, o_ref):
    del off_ref
    last = pl.num_programs(0) - 1

    @pl.when(pl.program_id(0) != last)
    def _():
        o_ref[...] = w_ref[...].astype(o_ref.dtype)

    @pl.when(pl.program_id(0) == last)
    def _():
        ki = w_ref[:IDX_DIM, :].astype(o_ref.dtype)
        o_ref[:IDX_DIM, :] = ki
        o_ref[IDX_DIM:2 * IDX_DIM, :] = ki
        o_ref[2 * IDX_DIM:2 * IDX_DIM + N_IDX_HEADS, :] = (
            w_ref[IDX_DIM:IDX_DIM + N_IDX_HEADS, :].astype(o_ref.dtype))
        o_ref[2 * IDX_DIM + N_IDX_HEADS:, :] = jnp.zeros(
            (WPREP_ROWS - 2 * IDX_DIM - N_IDX_HEADS, o_ref.shape[1]), o_ref.dtype)


def _reorder_w_in_t(w):
    d_in, moves = _column_moves()
    assert w.shape[1] == d_in and D_PAD - C_KK == WPREP_ROWS
    wt = jnp.swapaxes(w, 0, 1)
    offs = []
    for d, s0, width in sorted(moves):
        if d < C_KK:
            assert width % WPREP_ROWS == 0 and d == len(offs) * WPREP_ROWS
            offs += list(range(s0, s0 + width, WPREP_ROWS))
    offs.append(dict((d, s0) for d, s0, _ in moves)[C_KK])
    assert all(o % WPREP_ALIGN == 0 for o in offs)
    n_d = w.shape[0]
    return pl.pallas_call(
        _wprep_kernel,
        out_shape=jax.ShapeDtypeStruct((D_PAD, n_d), CDT),
        grid_spec=pltpu.PrefetchScalarGridSpec(
            num_scalar_prefetch=1,
            grid=(len(offs),),
            in_specs=[pl.BlockSpec((pl.Element(WPREP_ROWS), pl.Element(n_d)),
                                   lambda i, off: (off[i] * WPREP_ALIGN, 0))],
            out_specs=pl.BlockSpec((WPREP_ROWS, n_d), lambda i, off: (i, 0)),
        ),
        compiler_params=pltpu.CompilerParams(
            dimension_semantics=("parallel",), vmem_limit_bytes=VMEM_LIMIT),
        name="wprep",
    )(jnp.asarray([o // WPREP_ALIGN for o in offs], jnp.int32), wt)


def kernel(x, mem, g_in, w_in, b_gate, lam_q1, lam_k1, lam_q2, lam_k2, g_subln, g_mem,
           w_mem_kv, w_br_a, w_br_b, w_br_m, w_out, g_final):
    bsz, s_len, d = x.shape
    assert bsz == 1 and d == D_MODEL and g_in.shape[0] == 1
    topk = min(TOPK_MAX, s_len // 4)
    x2 = x.reshape(s_len, d)
    slopes = 2.0 ** (-8.0 * jnp.arange(1, N_HEADS_A + 1, dtype=F32) / N_HEADS_A)

    p = _proj(x2, g_in[0].reshape(1, d), _reorder_w_in_t(w_in[0]), tm=1024, tn=768)
    ua = _dsa(p, slopes, tq=512, tk=512, topk=topk)
    lamv = jnp.stack([lam_q1[0], lam_k1[0], lam_q2[0], lam_k2[0]]).astype(F32)
    ub = _diff(p, slopes, lamv, g_subln[0].reshape(1, 2 * DIFF_DIM), tq=512, tk=512)
    mkv = _memkv(mem.reshape(-1, d), g_mem[0].reshape(1, d), w_mem_kv[0], tn=512)
    um = _memattn(p, mkv, tm=1024)
    merged = _merge(ua, ub, um, p, b_gate[0].reshape(1, -1),
                    w_br_a[0].astype(CDT), w_br_b[0].astype(CDT), w_br_m[0].astype(CDT), tm=512)
    y = _out(x2, merged, w_out[0].astype(CDT), g_final.reshape(1, d), tm=512)
    return y.reshape(bsz, s_len, d)
```

```python
import functools

import jax
import jax.numpy as jnp
from jax import lax
from jax.experimental import pallas as pl
from jax.experimental.pallas import tpu as pltpu

F32 = jnp.float32
CDT = jnp.bfloat16

D_MODEL = 2048
N_HEADS_A = 8
HEAD_DIM_A = 128
N_IDX_HEADS = 16
IDX_DIM = 64
TOPK_MAX = 256
N_HEADS_B = 8
DIFF_DIM = 64
N_MEM_HEADS = 4
MEM_HEAD_DIM = 256
N_BRANCH = 3
WIDTH = 1024
RMS_EPS = 1e-6
NEG_INF = -1e30
F32_BIG = 3.0e38
LAM_INIT = 0.2
LANES = 128
LOG2E = 1.4426950408889634

C_GL = 0
C_QI = 6144
C_QA = 7168
C_KA = 8192
C_VA = 9216
C_ZA = 10240
C_QB = 11264
C_KB = 12288
C_VB = 13312
C_ZB = 14336
C_QM = 15360
C_ZM = 16384
C_KK = 17408
C_WI = 17536
D_PAD = 17664

VMEM_LIMIT = 56 * 1024 * 1024

BISECT_MAX_ITERS = 96
FLASH_PAIRS_PER_TRIP = 2
NORM_ROWS = 128
SOFT_ROWS = 16


def _nt_dot(a, b):
    return lax.dot_general(a, b, (((1,), (1,)), ((), ())), preferred_element_type=F32)


def _silu(z):
    return z * jax.nn.sigmoid(z)


def _proj_kernel(x_ref, g_ref, w_ref, o_ref, h_ref):
    @pl.when(pl.program_id(1) == 0)
    def _():
        for r in range(0, x_ref.shape[0], NORM_ROWS):
            xf = x_ref[r:r + NORM_ROWS, :]
            ms = jnp.mean(xf * xf, axis=-1, keepdims=True)
            h_ref[r:r + NORM_ROWS, :] = (xf * lax.rsqrt(ms + RMS_EPS) * g_ref[...]).astype(h_ref.dtype)

    o_ref[...] = _nt_dot(h_ref[...], w_ref[...]).astype(o_ref.dtype)


def _proj(x2, g, wt, tm, tn):
    s, d = x2.shape
    n = wt.shape[0]
    return pl.pallas_call(
        _proj_kernel,
        out_shape=jax.ShapeDtypeStruct((s, n), CDT),
        grid=(s // tm, n // tn),
        in_specs=[
            pl.BlockSpec((tm, d), lambda i, j: (i, 0)),
            pl.BlockSpec((1, d), lambda i, j: (0, 0)),
            pl.BlockSpec((tn, d), lambda i, j: (j, 0)),
        ],
        out_specs=pl.BlockSpec((tm, tn), lambda i, j: (i, j)),
        scratch_shapes=[pltpu.VMEM((tm, d), CDT)],
        compiler_params=pltpu.CompilerParams(
            dimension_semantics=("parallel", "arbitrary"), vmem_limit_bytes=VMEM_LIMIT),
        name="proj",
    )(x2, g, wt)


def _tile_lanes(x, n):
    return x if n == 1 else jnp.concatenate([x] * n, axis=1)


def _flash_scratch(rows, tk):
    return [pltpu.VMEM((rows, tk), F32), pltpu.VMEM((rows, tk), F32),
            pltpu.VMEM((rows, tk), CDT), pltpu.VMEM((rows, tk), CDT),
            pltpu.VMEM((rows, LANES), F32), pltpu.VMEM((rows, LANES), F32),
            pltpu.VMEM((rows, LANES), F32), pltpu.VMEM((rows, 2 * LANES), F32)]


def _flash_chunks(n_last, qk, v1, shift, valid, masked_pair, s_a, s_b, p_a, p_b, al_a, al_b, m_ref,
                  acc_ref):
    tk = s_a.shape[1]

    def soft(s_ref, p_ref, al_ref, c, masked=False):
        d = shift(c)
        for g in range(s_ref.shape[0] // SOFT_ROWS):
            rows = slice(g * SOFT_ROWS, (g + 1) * SOFT_ROWS)
            s = s_ref[rows, :]
            if masked and valid is not None:
                s = jnp.where(valid(rows, c), s, NEG_INF)
            m = m_ref[rows, :]
            m_new = jnp.maximum(m, jnp.max(s, axis=1, keepdims=True) - d)
            al_ref[rows, :] = jnp.exp2(m - m_new)
            m_ref[rows, :] = m_new
            p_ref[rows, :] = jnp.exp2(s - _tile_lanes(m_new + d, tk // LANES)).astype(p_ref.dtype)

    def pv(p_ref, al_ref, c):
        acc_ref[...] = (_tile_lanes(al_ref[...], 2) * acc_ref[...]
                        + jnp.dot(p_ref[...], v1(c), preferred_element_type=F32))

    m_ref[...] = jnp.full(m_ref.shape, NEG_INF, F32)
    acc_ref[...] = jnp.zeros(acc_ref.shape, F32)
    s_a[...] = qk(0)
    p_b[...] = jnp.zeros(p_b.shape, p_b.dtype)
    al_b[...] = jnp.ones(al_b.shape, F32)

    def pair(c0):
        pv(p_b, al_b, jnp.maximum(c0 - 1, 0))
        s_b[...] = qk(c0 + 1)
        soft(s_a, p_a, al_a, c0)
        pv(p_a, al_a, c0)
        s_a[...] = qk(c0 + 2)
        soft(s_b, p_b, al_b, c0 + 1)

    span = 2 * FLASH_PAIRS_PER_TRIP

    def multi_body(j, _):
        for u in range(FLASH_PAIRS_PER_TRIP):
            pair(span * j + 2 * u)
        return 0

    def pair_body(j, _):
        pair(span * (n_last // span) + 2 * j)
        return 0

    lax.fori_loop(0, n_last // span, multi_body, 0)
    lax.fori_loop(0, (n_last % span) // 2, pair_body, 0)

    def odd_tail():
        c0 = n_last - 1
        pv(p_b, al_b, jnp.maximum(c0 - 1, 0))
        s_b[...] = qk(n_last)
        soft(s_a, p_a, al_a, c0, masked=masked_pair)
        pv(p_a, al_a, c0)
        soft(s_b, p_b, al_b, n_last, masked=True)
        pv(p_b, al_b, n_last)

    def even_tail():
        pv(p_b, al_b, jnp.maximum(n_last - 1, 0))
        soft(s_a, p_a, al_a, n_last, masked=True)
        pv(p_a, al_a, n_last)

    if masked_pair:
        odd_tail()
    else:
        pl.when(n_last % 2 == 1)(odd_tail)
        pl.when(n_last % 2 == 0)(even_tail)


def _dsa_kernel(slopes_ref, qa_ref, ka_ref, va_ref, za_ref, qi_ref, kk_ref, wi_ref, o_ref,
                sc_ref, qz_ref, wb_ref, tb_ref, cnt_ref, *scratch, tq, tk, topk):
    qi = pl.program_id(0)
    h = pl.program_id(1)
    q0 = qi * tq
    n_full = (q0 + 1) // tk
    row = lax.broadcasted_iota(jnp.int32, (tq, tk), 0)
    col = lax.broadcasted_iota(jnp.int32, (tq, tk), 1)
    rc = row - col
    nsub = tk // LANES
    rblk = LANES

    def keys(kc):
        return pl.ds(pl.multiple_of(kc * tk, tk), tk)

    def fold(a, op):
        r = a[:, :LANES]
        for t in range(1, nsub):
            r = op(r, a[:, t * LANES:(t + 1) * LANES])
        return r

    @pl.when(h == 0)
    def _select():
        lane = lax.broadcasted_iota(jnp.int32, (tq, LANES), 1)
        for m in range(N_IDX_HEADS // 2):
            qp = qi_ref[:, m * LANES:(m + 1) * LANES].astype(F32)
            qz_ref[2 * m] = jnp.where(lane < IDX_DIM, qp, 0.0).astype(qz_ref.dtype)
            qz_ref[2 * m + 1] = jnp.where(lane >= IDX_DIM, qp, 0.0).astype(qz_ref.dtype)
        wi = wi_ref[...].astype(F32) * (IDX_DIM ** -0.5 * N_IDX_HEADS ** -0.5)
        for j in range(N_IDX_HEADS):
            wb_ref[j] = jnp.broadcast_to(wi[:, j:j + 1], (tq, LANES))

        def chunk_scores(kc):
            kk = kk_ref[keys(kc), :]
            acc = jnp.zeros((tq, tk), F32)
            for j in range(N_IDX_HEADS):
                acc = acc + jnp.maximum(_nt_dot(qz_ref[j], kk), 0.0) * _tile_lanes(wb_ref[j], nsub)
            return acc

        def full_body(kc, carry):
            rmin, rmax = carry
            a = chunk_scores(kc)
            sc_ref[kc] = a
            return jnp.minimum(rmin, fold(a, jnp.minimum)), jnp.maximum(rmax, fold(a, jnp.maximum))

        rmin, rmax = lax.fori_loop(
            0, n_full, full_body,
            (jnp.full((tq, LANES), F32_BIG, F32), jnp.full((tq, LANES), -F32_BIG, F32)))
        a = chunk_scores(n_full)
        causal = rc + (q0 - n_full * tk) >= 0
        sc_ref[n_full] = jnp.where(causal, a, -F32_BIG)
        rmin = jnp.minimum(rmin, fold(jnp.where(causal, a, F32_BIG), jnp.minimum))
        rmax = jnp.maximum(rmax, fold(jnp.where(causal, a, -F32_BIG), jnp.maximum))
        nblk = tq // rblk
        sub = lax.broadcasted_iota(jnp.int32, (8, LANES), 0)
        lane_id = lax.broadcasted_iota(jnp.int32, (8, LANES), 1)
        pad = [jnp.zeros((8 - nblk, LANES), F32)]

        def compact(x, reduce):
            return jnp.concatenate(
                [reduce(x[r * rblk:(r + 1) * rblk].T, axis=0, keepdims=True) for r in range(nblk)] + pad,
                axis=0)

        def expand(c):
            return jnp.concatenate(
                [jnp.broadcast_to(c[r:r + 1, :], (rblk, LANES)).T for r in range(nblk)], axis=0)

        sc_ref[n_full + 1] = jnp.full((tq, tk), -F32_BIG, F32)

        def count(tc, strict=False):
            tb_ref[...] = expand(tc)
            cnt_ref[...] = jnp.zeros(cnt_ref.shape, F32)

            def cbody(j, _):
                for g in range(tq // 8):
                    rows = slice(g * 8, (g + 1) * 8)
                    t = tb_ref[rows, :]
                    c = cnt_ref[rows, :]
                    for u in range(2):
                        x = sc_ref[2 * j + u, rows, :]
                        for l in range(nsub):
                            xl = x[:, l * LANES:(l + 1) * LANES]
                            c = c + jnp.where(xl > t if strict else xl >= t, 1.0, 0.0)
                    cnt_ref[rows, :] = c
                return 0

            lax.fori_loop(0, (n_full + 2) // 2, cbody, 0)
            return compact(cnt_ref[...], jnp.sum)

        kf = float(topk)
        n_causal = jnp.where(sub < nblk, q0 + sub * rblk + lane_id + 1, 0).astype(F32)
        lo0 = compact(rmin, jnp.min)
        hi0 = compact(rmax, jnp.max)
        hi0 = hi0 + (jnp.abs(hi0) * 1e-6 + 1e-30)

        def cond(c):
            return jnp.logical_and(c[0] < BISECT_MAX_ITERS, c[4])

        def body(c):
            it, lo, hi, cnt_lo, _ = c
            mid = lo + 0.5 * (hi - lo)
            cm = count(mid)
            ge = cm >= kf
            moving = jnp.logical_and(mid > lo, mid < hi)
            lo = jnp.where(ge, mid, lo)
            hi = jnp.where(ge, hi, mid)
            cnt_lo = jnp.where(ge, cm, cnt_lo)
            active = jnp.logical_and(cnt_lo > kf, moving)
            flag = jnp.max(jnp.where(active, 1.0, 0.0)) > 0.0
            return it + 1, lo, hi, cnt_lo, flag

        c0 = (jnp.int32(0), lo0, hi0, n_causal, jnp.max(n_causal) > kf)
        _, lo, _, cnt_lo, _ = lax.while_loop(cond, body, c0)
        lob = _tile_lanes(expand(lo), nsub)
        tied = jnp.max(cnt_lo) > kf

        @pl.when(jnp.logical_not(tied))
        def _():
            def mbody(kc, _):
                sc_ref[kc] = jnp.where(sc_ref[kc] >= lob, 0.0, NEG_INF)
                return 0

            lax.fori_loop(0, n_full + 1, mbody, 0)

        @pl.when(tied)
        def _():
            need = _tile_lanes(expand(kf - count(lo, strict=True)), nsub)
            ki = lax.broadcasted_iota(jnp.int32, (tk, tk), 0)
            kj = lax.broadcasted_iota(jnp.int32, (tk, tk), 1)
            prefix_ones = jnp.where(ki <= kj, 1.0, 0.0).astype(CDT)
            chunk_ones = jnp.ones((tk, LANES), CDT)

            def tbody(kc, seen):
                sc = sc_ref[kc]
                eq = sc == lob
                eqf = jnp.where(eq, 1.0, 0.0).astype(CDT)
                rank = jnp.dot(eqf, prefix_ones, preferred_element_type=F32) + _tile_lanes(seen, nsub)
                keep = jnp.logical_or(sc > lob, jnp.logical_and(eq, rank <= need))
                sc_ref[kc] = jnp.where(keep, 0.0, NEG_INF)
                return seen + jnp.dot(eqf, chunk_ones, preferred_element_type=F32)

            lax.fori_loop(0, n_full + 1, tbody, jnp.zeros((tq, LANES), F32))

    cs = slopes_ref[h] * LOG2E
    q = (qa_ref[...].astype(F32) * (HEAD_DIM_A ** -0.5 * LOG2E)).astype(qa_ref.dtype)
    colbias = cs * lax.broadcasted_iota(jnp.int32, (1, tk), 1).astype(F32)
    ones = jnp.ones((tk, LANES), va_ref.dtype)

    def qk(kc):
        return _nt_dot(q, ka_ref[keys(kc), :]) + colbias + sc_ref[kc]

    def v1(kc):
        return jnp.concatenate([va_ref[keys(kc), :], ones], axis=1)

    def shift(kc):
        return cs * (q0 - kc * tk).astype(F32)

    _flash_chunks(n_full, qk, v1, shift, None, False, *scratch)
    acc = scratch[-1][...]
    y = acc[:, :LANES] / acc[:, LANES:]
    o_ref[...] = (y * _silu(za_ref[...].astype(F32))).astype(o_ref.dtype)


def _dsa(p, slopes, tq, tk, topk):
    s = p.shape[0]
    nq = s // tq
    assert tk >= tq and s % tk == 0
    cb = lambda c: c // LANES
    kern = functools.partial(_dsa_kernel, tq=tq, tk=tk, topk=topk)
    return pl.pallas_call(
        kern,
        out_shape=jax.ShapeDtypeStruct((s, WIDTH), CDT),
        grid=(nq, N_HEADS_A),
        in_specs=[
            pl.BlockSpec(memory_space=pltpu.SMEM),
            pl.BlockSpec((tq, LANES), lambda i, h: (i, cb(C_QA) + h)),
            pl.BlockSpec((s, LANES), lambda i, h: (0, cb(C_KA) + h)),
            pl.BlockSpec((s, LANES), lambda i, h: (0, cb(C_VA) + h)),
            pl.BlockSpec((tq, LANES), lambda i, h: (i, cb(C_ZA) + h)),
            pl.BlockSpec((tq, WIDTH), lambda i, h: (i, C_QI // WIDTH)),
            pl.BlockSpec((s, LANES), lambda i, h: (0, cb(C_KK))),
            pl.BlockSpec((tq, LANES), lambda i, h: (i, cb(C_WI))),
        ],
        out_specs=pl.BlockSpec((tq, LANES), lambda i, h: (i, h)),
        scratch_shapes=[
            pltpu.VMEM((s // tk + 1, tq, tk), F32),
            pltpu.VMEM((N_IDX_HEADS, tq, LANES), CDT),
            pltpu.VMEM((N_IDX_HEADS, tq, LANES), F32),
            pltpu.VMEM((tq, LANES), F32),
            pltpu.VMEM((tq, LANES), F32),
        ] + _flash_scratch(tq, tk),
        compiler_params=pltpu.CompilerParams(
            dimension_semantics=("parallel", "arbitrary"), vmem_limit_bytes=VMEM_LIMIT),
        name="dsa",
    )(slopes, p, p, p, p, p, p, p)


def _diff_kernel(slopes_ref, lam_ref, qb_ref, kb_ref, vb_ref, zb_ref, g_ref, o_ref, *scratch, tq, tk):
    h = pl.program_id(0)
    qi = pl.program_id(1)
    q0 = qi * tq
    cs = slopes_ref[h] * LOG2E
    lane = lax.broadcasted_iota(jnp.int32, (tq, LANES), 1)
    qf = qb_ref[...].astype(F32) * (DIFF_DIM ** -0.5 * LOG2E)
    qs = jnp.concatenate([jnp.where(lane < DIFF_DIM, qf, 0.0),
                          jnp.where(lane >= DIFF_DIM, qf, 0.0)], axis=0).astype(qb_ref.dtype)
    colbias = cs * lax.broadcasted_iota(jnp.int32, (1, tk), 1).astype(F32)
    ones = jnp.ones((tk, LANES), vb_ref.dtype)

    def keys(kc):
        return pl.ds(pl.multiple_of(kc * tk, tk), tk)

    def qk(kc):
        return _nt_dot(qs, kb_ref[keys(kc), :]) + colbias

    def v1(kc):
        return jnp.concatenate([vb_ref[keys(kc), :], ones], axis=1)

    def shift(kc):
        return cs * (q0 - kc * tk).astype(F32)

    n_last = (q0 + tq - 1) // tk

    def valid(rows, c):
        r = lax.broadcasted_iota(jnp.int32, (SOFT_ROWS, tk), 0) + rows.start % tq
        col = lax.broadcasted_iota(jnp.int32, (SOFT_ROWS, tk), 1)
        return r - col + (q0 - c * tk) >= 0

    _flash_chunks(n_last, qk, v1, shift, valid, tq == 2 * tk, *scratch)
    acc = scratch[-1][...]

    lv = lam_ref[...]
    lam = (jnp.exp(jnp.sum(lv[0:1] * lv[1:2], axis=1, keepdims=True))
           - jnp.exp(jnp.sum(lv[2:3] * lv[3:4], axis=1, keepdims=True)) + LAM_INIT)
    y = (acc[:tq, :LANES] / acc[:tq, LANES:]) - lam * (acc[tq:, :LANES] / acc[tq:, LANES:])
    ms = jnp.mean(y * y, axis=-1, keepdims=True)
    y = (y * lax.rsqrt(ms + RMS_EPS) * g_ref[...]) * (1.0 - LAM_INIT)
    o_ref[...] = (y * _silu(zb_ref[...].astype(F32))).astype(o_ref.dtype)


def _diff(p, slopes, lamv, g_subln, tq, tk):
    s = p.shape[0]
    nq = s // tq
    assert tq in (tk, 2 * tk) and s % tq == 0
    cb = lambda c: c // LANES
    kern = functools.partial(_diff_kernel, tq=tq, tk=tk)
    return pl.pallas_call(
        kern,
        out_shape=jax.ShapeDtypeStruct((s, WIDTH), CDT),
        grid=(N_HEADS_B, nq),
        in_specs=[
            pl.BlockSpec(memory_space=pltpu.SMEM),
            pl.BlockSpec((4, DIFF_DIM), lambda h, i: (0, 0)),
            pl.BlockSpec((tq, LANES), lambda h, i: (i, cb(C_QB) + h)),
            pl.BlockSpec((s, LANES), lambda h, i: (0, cb(C_KB) + h)),
            pl.BlockSpec((s, LANES), lambda h, i: (0, cb(C_VB) + h)),
            pl.BlockSpec((tq, LANES), lambda h, i: (i, cb(C_ZB) + h)),
            pl.BlockSpec((1, LANES), lambda h, i: (0, 0)),
        ],
        out_specs=pl.BlockSpec((tq, LANES), lambda h, i: (i, h)),
        scratch_shapes=_flash_scratch(2 * tq, tk),
        compiler_params=pltpu.CompilerParams(
            dimension_semantics=("parallel", "parallel"), vmem_limit_bytes=VMEM_LIMIT),
        name="diff",
    )(slopes, lamv, p, p, p, p, g_subln)


def _memkv_kernel(mem_ref, g_ref, w_ref, o_ref, h_ref):
    @pl.when(pl.program_id(0) == 0)
    def _():
        xf = mem_ref[...]
        ms = jnp.mean(xf * xf, axis=-1, keepdims=True)
        h_ref[...] = (xf * lax.rsqrt(ms + RMS_EPS) * g_ref[...]).astype(h_ref.dtype)

    o_ref[...] = jnp.dot(h_ref[...], w_ref[...].astype(h_ref.dtype),
                         preferred_element_type=F32).astype(o_ref.dtype)


def _memkv(mem2, g, w, tn):
    nm, d = mem2.shape
    n = w.shape[1]
    return pl.pallas_call(
        _memkv_kernel,
        out_shape=jax.ShapeDtypeStruct((nm, n), CDT),
        grid=(n // tn,),
        in_specs=[
            pl.BlockSpec((nm, d), lambda j: (0, 0)),
            pl.BlockSpec((1, d), lambda j: (0, 0)),
            pl.BlockSpec((d, tn), lambda j: (0, j)),
        ],
        out_specs=pl.BlockSpec((nm, tn), lambda j: (0, j)),
        scratch_shapes=[pltpu.VMEM((nm, d), CDT)],
        compiler_params=pltpu.CompilerParams(
            dimension_semantics=("arbitrary",), vmem_limit_bytes=VMEM_LIMIT),
        name="memkv",
    )(mem2, g, w)


def _mem_branch(q_ref, z_ref, kv_ref):
    outs = []
    for hh in range(N_MEM_HEADS):
        cols = slice(hh * MEM_HEAD_DIM, (hh + 1) * MEM_HEAD_DIM)
        vcols = slice(WIDTH + hh * MEM_HEAD_DIM, WIDTH + (hh + 1) * MEM_HEAD_DIM)
        s = _nt_dot(q_ref[:, cols], kv_ref[:, cols]) * (MEM_HEAD_DIM ** -0.5)
        m = jnp.max(s, axis=1, keepdims=True)
        p = jnp.exp(s - m)
        l = jnp.sum(p, axis=1, keepdims=True)
        y = jnp.dot(p.astype(kv_ref.dtype), kv_ref[:, vcols], preferred_element_type=F32) / l
        outs.append((y * _silu(z_ref[:, cols].astype(F32))).astype(kv_ref.dtype))
    return jnp.concatenate(outs, axis=1)


def _merge_kernel(ua_ref, ub_ref, qm_ref, zm_ref, mkv_ref, ga_ref, gb_ref, gm_ref, b_ref,
                  wa_ref, wb_ref, wm_ref, o_ref):
    d = o_ref.shape[1]
    um = _mem_branch(qm_ref, zm_ref, mkv_ref)
    acc = None
    for t, (u, g_ref, w_ref) in enumerate(
            ((ua_ref[...], ga_ref, wa_ref), (ub_ref[...], gb_ref, wb_ref), (um, gm_ref, wm_ref))):
        gate = jax.nn.sigmoid(g_ref[...].astype(F32) + b_ref[:, t * d:(t + 1) * d])
        term = gate * jnp.dot(u, w_ref[...], preferred_element_type=F32)
        acc = term if acc is None else acc + term
    o_ref[...] = acc.astype(o_ref.dtype)


def _merge(ua, ub, mkv, p, b_gate, wa, wb, wm, tm):
    s = ua.shape[0]
    d = D_MODEL
    once = pl.Buffered(1)
    u_spec = pl.BlockSpec((tm, WIDTH), lambda i: (i, 0))
    w_spec = pl.BlockSpec((WIDTH, d), lambda i: (0, 0), pipeline_mode=once)
    return pl.pallas_call(
        _merge_kernel,
        out_shape=jax.ShapeDtypeStruct((s, d), CDT),
        grid=(s // tm,),
        in_specs=[
            u_spec, u_spec,
            pl.BlockSpec((tm, WIDTH), lambda i: (i, C_QM // WIDTH)),
            pl.BlockSpec((tm, WIDTH), lambda i: (i, C_ZM // WIDTH)),
            pl.BlockSpec(mkv.shape, lambda i: (0, 0), pipeline_mode=once),
            pl.BlockSpec((tm, d), lambda i: (i, C_GL // d + 0)),
            pl.BlockSpec((tm, d), lambda i: (i, C_GL // d + 1)),
            pl.BlockSpec((tm, d), lambda i: (i, C_GL // d + 2)),
            pl.BlockSpec((1, N_BRANCH * d), lambda i: (0, 0)),
            w_spec, w_spec, w_spec,
        ],
        out_specs=pl.BlockSpec((tm, d), lambda i: (i, 0)),
        compiler_params=pltpu.CompilerParams(
            dimension_semantics=("parallel",), vmem_limit_bytes=VMEM_LIMIT),
        name="merge",
    )(ua, ub, p, p, mkv, p, p, p, b_gate, wa, wb, wm)


def _out_kernel(x_ref, mg_ref, w_ref, g_ref, o_ref):
    y = x_ref[...] + jnp.dot(mg_ref[...], w_ref[...], preferred_element_type=F32)
    ms = jnp.mean(y * y, axis=-1, keepdims=True)
    o_ref[...] = (y * lax.rsqrt(ms + RMS_EPS) * g_ref[...]).astype(o_ref.dtype)


def _out(x2, merged, w_out, g_final, tm):
    s, d = x2.shape
    return pl.pallas_call(
        _out_kernel,
        out_shape=jax.ShapeDtypeStruct((s, d), x2.dtype),
        grid=(s // tm,),
        in_specs=[
            pl.BlockSpec((tm, d), lambda i: (i, 0)),
            pl.BlockSpec((tm, d), lambda i: (i, 0)),
            pl.BlockSpec((d, d), lambda i: (0, 0), pipeline_mode=pl.Buffered(1)),
            pl.BlockSpec((1, d), lambda i: (0, 0)),
        ],
        out_specs=pl.BlockSpec((tm, d), lambda i: (i, 0)),
        compiler_params=pltpu.CompilerParams(
            dimension_semantics=("parallel",), vmem_limit_bytes=VMEM_LIMIT),
        name="out",
    )(x2, merged, w_out, g_final)


def _column_moves():
    src = {}
    o = 0
    for name, width in (("qa", WIDTH), ("ka", WIDTH), ("va", WIDTH), ("za", WIDTH),
                        ("qi", N_IDX_HEADS * IDX_DIM), ("ki", IDX_DIM), ("wi", N_IDX_HEADS),
                        ("qb", WIDTH), ("kb", WIDTH), ("vb", WIDTH), ("zb", WIDTH),
                        ("qm", WIDTH), ("zm", WIDTH), ("gl", N_BRANCH * D_MODEL)):
        src[name] = (o, width)
        o += width
    dst = (("gl", C_GL), ("qi", C_QI), ("qa", C_QA), ("ka", C_KA), ("va", C_VA), ("za", C_ZA),
           ("qb", C_QB), ("kb", C_KB), ("vb", C_VB), ("zb", C_ZB), ("qm", C_QM), ("zm", C_ZM),
           ("ki", C_KK), ("ki", C_KK + IDX_DIM), ("wi", C_WI))
    return o, [(d,) + src[name] for name, d in dst]


WPREP_ROWS = 256
WPREP_ALIGN = 16


def _wprep_kernel(off_ref, w_ref, o_ref):
    del off_ref
    last = pl.num_programs(0) - 1

    @pl.when(pl.program_id(0) != last)
    def _():
        o_ref[...] = w_ref[...].astype(o_ref.dtype)

    @pl.when(pl.program_id(0) == last)
    def _():
        ki = w_ref[:IDX_DIM, :].astype(o_ref.dtype)
        o_ref[:IDX_DIM, :] = ki
        o_ref[IDX_DIM:2 * IDX_DIM, :] = ki
        o_ref[2 * IDX_DIM:2 * IDX_DIM + N_IDX_HEADS, :] = (
            w_ref[IDX_DIM:IDX_DIM + N_IDX_HEADS, :].astype(o_ref.dtype))
        o_ref[2 * IDX_DIM + N_IDX_HEADS:, :] = jnp.zeros(
            (WPREP_ROWS - 2 * IDX_DIM - N_IDX_HEADS, o_ref.shape[1]), o_ref.dtype)


def _reorder_w_in_t(w):
    d_in, moves = _column_moves()
    assert w.shape[1] == d_in and D_PAD - C_KK == WPREP_ROWS
    wt = jnp.swapaxes(w, 0, 1)
    offs = []
    for d, s0, width in sorted(moves):
        if d < C_KK:
            assert width % WPREP_ROWS == 0 and d == len(offs) * WPREP_ROWS
            offs += list(range(s0, s0 + width, WPREP_ROWS))
    offs.append(dict((d, s0) for d, s0, _ in moves)[C_KK])
    assert all(o % WPREP_ALIGN == 0 for o in offs)
    n_d = w.shape[0]
    return pl.pallas_call(
        _wprep_kernel,
        out_shape=jax.ShapeDtypeStruct((D_PAD, n_d), CDT),
        grid_spec=pltpu.PrefetchScalarGridSpec(
            num_scalar_prefetch=1,
            grid=(len(offs),),
            in_specs=[pl.BlockSpec((pl.Element(WPREP_ROWS), pl.Element(n_d)),
                                   lambda i, off: (off[i] * WPREP_ALIGN, 0))],
            out_specs=pl.BlockSpec((WPREP_ROWS, n_d), lambda i, off: (i, 0)),
        ),
        compiler_params=pltpu.CompilerParams(
            dimension_semantics=("parallel",), vmem_limit_bytes=VMEM_LIMIT),
        name="wprep",
    )(jnp.asarray([o // WPREP_ALIGN for o in offs], jnp.int32), wt)


def kernel(x, mem, g_in, w_in, b_gate, lam_q1, lam_k1, lam_q2, lam_k2, g_subln, g_mem,
           w_mem_kv, w_br_a, w_br_b, w_br_m, w_out, g_final):
    bsz, s_len, d = x.shape
    assert bsz == 1 and d == D_MODEL and g_in.shape[0] == 1
    topk = min(TOPK_MAX, s_len // 4)
    x2 = x.reshape(s_len, d)
    slopes = 2.0 ** (-8.0 * jnp.arange(1, N_HEADS_A + 1, dtype=F32) / N_HEADS_A)

    p = _proj(x2, g_in[0].reshape(1, d), _reorder_w_in_t(w_in[0]), tm=1024, tn=768)
    ua = _dsa(p, slopes, tq=512, tk=512, topk=topk)
    lamv = jnp.stack([lam_q1[0], lam_k1[0], lam_q2[0], lam_k2[0]]).astype(F32)
    ub = _diff(p, slopes, lamv, g_subln[0].reshape(1, 2 * DIFF_DIM), tq=512, tk=512)
    mkv = _memkv(mem.reshape(-1, d), g_mem[0].reshape(1, d), w_mem_kv[0], tn=512)
    merged = _merge(ua, ub, mkv, p, b_gate[0].reshape(1, -1),
                    w_br_a[0].astype(CDT), w_br_b[0].astype(CDT), w_br_m[0].astype(CDT), tm=512)
    y = _out(x2, merged, w_out[0].astype(CDT), g_final.reshape(1, d), tm=512)
    return y.reshape(bsz, s_len, d)
```

```python
import functools

import jax
import jax.numpy as jnp
from jax import lax
from jax.experimental import pallas as pl
from jax.experimental.pallas import tpu as pltpu

F32 = jnp.float32
CDT = jnp.bfloat16

D_MODEL = 2048
N_HEADS_A = 8
HEAD_DIM_A = 128
N_IDX_HEADS = 16
IDX_DIM = 64
TOPK_MAX = 256
N_HEADS_B = 8
DIFF_DIM = 64
N_MEM_HEADS = 4
MEM_HEAD_DIM = 256
N_BRANCH = 3
WIDTH = 1024
RMS_EPS = 1e-6
NEG_INF = -1e30
F32_BIG = 3.0e38
LAM_INIT = 0.2
LANES = 128
LOG2E = 1.4426950408889634

C_GL = 0
C_QI = 6144
C_QA = 7168
C_KA = 8192
C_VA = 9216
C_ZA = 10240
C_QB = 11264
C_KB = 12288
C_VB = 13312
C_ZB = 14336
C_QM = 15360
C_ZM = 16384
C_KK = 17408
C_WI = 17536
D_PAD = 17664

VMEM_LIMIT = 56 * 1024 * 1024

BISECT_MAX_ITERS = 96
FLASH_PAIRS_PER_TRIP = 2
NORM_ROWS = 128
SOFT_ROWS = 16


def _nt_dot(a, b):
    return lax.dot_general(a, b, (((1,), (1,)), ((), ())), preferred_element_type=F32)


def _silu(z):
    return z * jax.nn.sigmoid(z)


def _proj_kernel(x_ref, g_ref, w_ref, o_ref, h_ref):
    @pl.when(pl.program_id(1) == 0)
    def _():
        for r in range(0, x_ref.shape[0], NORM_ROWS):
            xf = x_ref[r:r + NORM_ROWS, :]
            ms = jnp.mean(xf * xf, axis=-1, keepdims=True)
            h_ref[r:r + NORM_ROWS, :] = (xf * lax.rsqrt(ms + RMS_EPS) * g_ref[...]).astype(h_ref.dtype)

    o_ref[...] = _nt_dot(h_ref[...], w_ref[...]).astype(o_ref.dtype)


def _proj(x2, g, wt, tm, tn):
    s, d = x2.shape
    n = wt.shape[0]
    return pl.pallas_call(
        _proj_kernel,
        out_shape=jax.ShapeDtypeStruct((s, n), CDT),
        grid=(s // tm, n // tn),
        in_specs=[
            pl.BlockSpec((tm, d), lambda i, j: (i, 0)),
            pl.BlockSpec((1, d), lambda i, j: (0, 0)),
            pl.BlockSpec((tn, d), lambda i, j: (j, 0)),
        ],
        out_specs=pl.BlockSpec((tm, tn), lambda i, j: (i, j)),
        scratch_shapes=[pltpu.VMEM((tm, d), CDT)],
        compiler_params=pltpu.CompilerParams(
            dimension_semantics=("parallel", "arbitrary"), vmem_limit_bytes=VMEM_LIMIT),
        name="proj",
    )(x2, g, wt)


def _tile_lanes(x, n):
    return x if n == 1 else jnp.concatenate([x] * n, axis=1)


def _flash_scratch(rows, tk):
    return [pltpu.VMEM((rows, tk), F32), pltpu.VMEM((rows, tk), F32),
            pltpu.VMEM((rows, tk), CDT), pltpu.VMEM((rows, tk), CDT),
            pltpu.VMEM((rows, LANES), F32), pltpu.VMEM((rows, LANES), F32),
            pltpu.VMEM((rows, LANES), F32), pltpu.VMEM((rows, 2 * LANES), F32)]


def _flash_chunks(n_last, qk, v1, shift, valid, masked_pair, s_a, s_b, p_a, p_b, al_a, al_b, m_ref,
                  acc_ref):
    tk = s_a.shape[1]

    def soft(s_ref, p_ref, al_ref, c, masked=False):
        d = shift(c)
        for g in range(s_ref.shape[0] // SOFT_ROWS):
            rows = slice(g * SOFT_ROWS, (g + 1) * SOFT_ROWS)
            s = s_ref[rows, :]
            if masked and valid is not None:
                s = jnp.where(valid(rows, c), s, NEG_INF)
            m = m_ref[rows, :]
            m_new = jnp.maximum(m, jnp.max(s, axis=1, keepdims=True) - d)
            al_ref[rows, :] = jnp.exp2(m - m_new)
            m_ref[rows, :] = m_new
            p_ref[rows, :] = jnp.exp2(s - _tile_lanes(m_new + d, tk // LANES)).astype(p_ref.dtype)

    def pv(p_ref, al_ref, c):
        acc_ref[...] = (_tile_lanes(al_ref[...], 2) * acc_ref[...]
                        + jnp.dot(p_ref[...], v1(c), preferred_element_type=F32))

    m_ref[...] = jnp.full(m_ref.shape, NEG_INF, F32)
    acc_ref[...] = jnp.zeros(acc_ref.shape, F32)
    s_a[...] = qk(0)

    def pair(c0, lead=True):
        if lead:
            pv(p_b, al_b, c0 - 1)
        s_b[...] = qk(c0 + 1)
        soft(s_a, p_a, al_a, c0)
        pv(p_a, al_a, c0)
        s_a[...] = qk(c0 + 2)
        soft(s_b, p_b, al_b, c0 + 1)

    span = 2 * FLASH_PAIRS_PER_TRIP
    trips = n_last // span
    rest = (n_last % span) // 2

    @pl.when(trips > 0)
    def _():
        pair(0, lead=False)
        for u in range(1, FLASH_PAIRS_PER_TRIP):
            pair(2 * u)

    def multi_body(j, _):
        for u in range(FLASH_PAIRS_PER_TRIP):
            pair(span * j + 2 * u)
        return 0

    lax.fori_loop(1, trips, multi_body, 0)

    @pl.when(jnp.logical_and(trips == 0, rest > 0))
    def _():
        pair(0, lead=False)

    def pair_body(j, _):
        pair(span * trips + 2 * j)
        return 0

    lax.fori_loop(jnp.where(trips == 0, 1, 0), rest, pair_body, 0)

    def odd_tail(lead):
        c0 = n_last - 1
        if lead:
            pv(p_b, al_b, c0 - 1)
        s_b[...] = qk(n_last)
        soft(s_a, p_a, al_a, c0, masked=masked_pair)
        pv(p_a, al_a, c0)
        soft(s_b, p_b, al_b, n_last, masked=True)
        pv(p_b, al_b, n_last)

    def even_tail(lead):
        if lead:
            pv(p_b, al_b, n_last - 1)
        soft(s_a, p_a, al_a, n_last, masked=True)
        pv(p_a, al_a, n_last)

    odd = n_last % 2 == 1
    pl.when(jnp.logical_and(odd, n_last > 1))(functools.partial(odd_tail, True))
    pl.when(n_last == 1)(functools.partial(odd_tail, False))
    if not masked_pair:
        pl.when(jnp.logical_and(jnp.logical_not(odd), n_last > 0))(functools.partial(even_tail, True))
        pl.when(n_last == 0)(functools.partial(even_tail, False))


def _dsa_kernel(slopes_ref, qa_ref, ka_ref, va_ref, za_ref, qi_ref, kk_ref, wi_ref, o_ref,
                sc_ref, qz_ref, wb_ref, tb_ref, cnt_ref, *scratch, tq, tk, topk):
    qi = pl.program_id(0)
    h = pl.program_id(1)
    q0 = qi * tq
    n_full = (q0 + 1) // tk
    row = lax.broadcasted_iota(jnp.int32, (tq, tk), 0)
    col = lax.broadcasted_iota(jnp.int32, (tq, tk), 1)
    rc = row - col
    nsub = tk // LANES
    rblk = LANES

    def keys(kc):
        return pl.ds(pl.multiple_of(kc * tk, tk), tk)

    def fold(a, op):
        r = a[:, :LANES]
        for t in range(1, nsub):
            r = op(r, a[:, t * LANES:(t + 1) * LANES])
        return r

    @pl.when(h == 0)
    def _select():
        lane = lax.broadcasted_iota(jnp.int32, (tq, LANES), 1)
        for m in range(N_IDX_HEADS // 2):
            qp = qi_ref[:, m * LANES:(m + 1) * LANES].astype(F32)
            qz_ref[2 * m] = jnp.where(lane < IDX_DIM, qp, 0.0).astype(qz_ref.dtype)
            qz_ref[2 * m + 1] = jnp.where(lane >= IDX_DIM, qp, 0.0).astype(qz_ref.dtype)
        wi = wi_ref[...].astype(F32) * (IDX_DIM ** -0.5 * N_IDX_HEADS ** -0.5)
        for j in range(N_IDX_HEADS):
            wb_ref[j] = jnp.broadcast_to(wi[:, j:j + 1], (tq, LANES))

        def chunk_scores(kc):
            kk = kk_ref[keys(kc), :]
            acc = jnp.zeros((tq, tk), F32)
            for j in range(N_IDX_HEADS):
                acc = acc + jnp.maximum(_nt_dot(qz_ref[j], kk), 0.0) * _tile_lanes(wb_ref[j], nsub)
            return acc

        def full_body(kc, carry):
            rmin, rmax = carry
            a = chunk_scores(kc)
            sc_ref[kc] = a
            return jnp.minimum(rmin, fold(a, jnp.minimum)), jnp.maximum(rmax, fold(a, jnp.maximum))

        rmin, rmax = lax.fori_loop(
            0, n_full, full_body,
            (jnp.full((tq, LANES), F32_BIG, F32), jnp.full((tq, LANES), -F32_BIG, F32)))
        a = chunk_scores(n_full)
        causal = rc + (q0 - n_full * tk) >= 0
        sc_ref[n_full] = jnp.where(causal, a, -F32_BIG)
        rmin = jnp.minimum(rmin, fold(jnp.where(causal, a, F32_BIG), jnp.minimum))
        rmax = jnp.maximum(rmax, fold(jnp.where(causal, a, -F32_BIG), jnp.maximum))
        nblk = tq // rblk
        sub = lax.broadcasted_iota(jnp.int32, (8, LANES), 0)
        lane_id = lax.broadcasted_iota(jnp.int32, (8, LANES), 1)
        pad = [jnp.zeros((8 - nblk, LANES), F32)]

        def compact(x, reduce):
            return jnp.concatenate(
                [reduce(x[r * rblk:(r + 1) * rblk].T, axis=0, keepdims=True) for r in range(nblk)] + pad,
                axis=0)

        def expand(c):
            return jnp.concatenate(
                [jnp.broadcast_to(c[r:r + 1, :], (rblk, LANES)).T for r in range(nblk)], axis=0)

        sc_ref[n_full + 1] = jnp.full((tq, tk), -F32_BIG, F32)

        def count(tc, strict=False):
            tb_ref[...] = expand(tc)
            cnt_ref[...] = jnp.zeros(cnt_ref.shape, F32)

            def cbody(j, _):
                for g in range(tq // 8):
                    rows = slice(g * 8, (g + 1) * 8)
                    t = tb_ref[rows, :]
                    c = cnt_ref[rows, :]
                    for u in range(2):
                        x = sc_ref[2 * j + u, rows, :]
                        for l in range(nsub):
                            xl = x[:, l * LANES:(l + 1) * LANES]
                            c = c + jnp.where(xl > t if strict else xl >= t, 1.0, 0.0)
                    cnt_ref[rows, :] = c
                return 0

            lax.fori_loop(0, (n_full + 2) // 2, cbody, 0)
            return compact(cnt_ref[...], jnp.sum)

        kf = float(topk)
        n_causal = jnp.where(sub < nblk, q0 + sub * rblk + lane_id + 1, 0).astype(F32)
        lo0 = compact(rmin, jnp.min)
        hi0 = compact(rmax, jnp.max)
        hi0 = hi0 + (jnp.abs(hi0) * 1e-6 + 1e-30)

        def cond(c):
            return jnp.logical_and(c[0] < BISECT_MAX_ITERS, c[4])

        def body(c):
            it, lo, hi, cnt_lo, _ = c
            mid = lo + 0.5 * (hi - lo)
            cm = count(mid)
            ge = cm >= kf
            moving = jnp.logical_and(mid > lo, mid < hi)
            lo = jnp.where(ge, mid, lo)
            hi = jnp.where(ge, hi, mid)
            cnt_lo = jnp.where(ge, cm, cnt_lo)
            active = jnp.logical_and(cnt_lo > kf, moving)
            flag = jnp.max(jnp.where(active, 1.0, 0.0)) > 0.0
            return it + 1, lo, hi, cnt_lo, flag

        c0 = (jnp.int32(0), lo0, hi0, n_causal, jnp.max(n_causal) > kf)
        _, lo, _, cnt_lo, _ = lax.while_loop(cond, body, c0)
        lob = _tile_lanes(expand(lo), nsub)
        tied = jnp.max(cnt_lo) > kf

        @pl.when(jnp.logical_not(tied))
        def _():
            def mbody(kc, _):
                sc_ref[kc] = jnp.where(sc_ref[kc] >= lob, 0.0, NEG_INF)
                return 0

            lax.fori_loop(0, n_full + 1, mbody, 0)

        @pl.when(tied)
        def _():
            need = _tile_lanes(expand(kf - count(lo, strict=True)), nsub)
            ki = lax.broadcasted_iota(jnp.int32, (tk, tk), 0)
            kj = lax.broadcasted_iota(jnp.int32, (tk, tk), 1)
            prefix_ones = jnp.where(ki <= kj, 1.0, 0.0).astype(CDT)
            chunk_ones = jnp.ones((tk, LANES), CDT)

            def tbody(kc, seen):
                sc = sc_ref[kc]
                eq = sc == lob
                eqf = jnp.where(eq, 1.0, 0.0).astype(CDT)
                rank = jnp.dot(eqf, prefix_ones, preferred_element_type=F32) + _tile_lanes(seen, nsub)
                keep = jnp.logical_or(sc > lob, jnp.logical_and(eq, rank <= need))
                sc_ref[kc] = jnp.where(keep, 0.0, NEG_INF)
                return seen + jnp.dot(eqf, chunk_ones, preferred_element_type=F32)

            lax.fori_loop(0, n_full + 1, tbody, jnp.zeros((tq, LANES), F32))

    cs = slopes_ref[h] * LOG2E
    q = (qa_ref[...].astype(F32) * (HEAD_DIM_A ** -0.5 * LOG2E)).astype(qa_ref.dtype)
    colbias = cs * lax.broadcasted_iota(jnp.int32, (1, tk), 1).astype(F32)
    ones = jnp.ones((tk, LANES), va_ref.dtype)

    def qk(kc):
        return _nt_dot(q, ka_ref[keys(kc), :]) + colbias + sc_ref[kc]

    def v1(kc):
        return jnp.concatenate([va_ref[keys(kc), :], ones], axis=1)

    def shift(kc):
        return cs * (q0 - kc * tk).astype(F32)

    _flash_chunks(n_full, qk, v1, shift, None, False, *scratch)
    acc = scratch[-1][...]
    y = acc[:, :LANES] / acc[:, LANES:]
    o_ref[...] = (y * _silu(za_ref[...].astype(F32))).astype(o_ref.dtype)


def _dsa(p, slopes, tq, tk, topk):
    s = p.shape[0]
    nq = s // tq
    assert tk >= tq and s % tk == 0
    cb = lambda c: c // LANES
    kern = functools.partial(_dsa_kernel, tq=tq, tk=tk, topk=topk)
    return pl.pallas_call(
        kern,
        out_shape=jax.ShapeDtypeStruct((s, WIDTH), CDT),
        grid=(nq, N_HEADS_A),
        in_specs=[
            pl.BlockSpec(memory_space=pltpu.SMEM),
            pl.BlockSpec((tq, LANES), lambda i, h: (i, cb(C_QA) + h)),
            pl.BlockSpec((s, LANES), lambda i, h: (0, cb(C_KA) + h)),
            pl.BlockSpec((s, LANES), lambda i, h: (0, cb(C_VA) + h)),
            pl.BlockSpec((tq, LANES), lambda i, h: (i, cb(C_ZA) + h)),
            pl.BlockSpec((tq, WIDTH), lambda i, h: (i, C_QI // WIDTH)),
            pl.BlockSpec((s, LANES), lambda i, h: (0, cb(C_KK))),
            pl.BlockSpec((tq, LANES), lambda i, h: (i, cb(C_WI))),
        ],
        out_specs=pl.BlockSpec((tq, LANES), lambda i, h: (i, h)),
        scratch_shapes=[
            pltpu.VMEM((s // tk + 1, tq, tk), F32),
            pltpu.VMEM((N_IDX_HEADS, tq, LANES), CDT),
            pltpu.VMEM((N_IDX_HEADS, tq, LANES), F32),
            pltpu.VMEM((tq, LANES), F32),
            pltpu.VMEM((tq, LANES), F32),
        ] + _flash_scratch(tq, tk),
        compiler_params=pltpu.CompilerParams(
            dimension_semantics=("parallel", "arbitrary"), vmem_limit_bytes=VMEM_LIMIT),
        name="dsa",
    )(slopes, p, p, p, p, p, p, p)


def _diff_kernel(slopes_ref, lam_ref, qb_ref, kb_ref, vb_ref, zb_ref, g_ref, o_ref, *scratch, tq, tk):
    h = pl.program_id(0)
    qi = pl.program_id(1)
    q0 = qi * tq
    cs = slopes_ref[h] * LOG2E
    lane = lax.broadcasted_iota(jnp.int32, (tq, LANES), 1)
    qf = qb_ref[...].astype(F32) * (DIFF_DIM ** -0.5 * LOG2E)
    qs = jnp.concatenate([jnp.where(lane < DIFF_DIM, qf, 0.0),
                          jnp.where(lane >= DIFF_DIM, qf, 0.0)], axis=0).astype(qb_ref.dtype)
    colbias = cs * lax.broadcasted_iota(jnp.int32, (1, tk), 1).astype(F32)
    ones = jnp.ones((tk, LANES), vb_ref.dtype)

    def keys(kc):
        return pl.ds(pl.multiple_of(kc * tk, tk), tk)

    def qk(kc):
        return _nt_dot(qs, kb_ref[keys(kc), :]) + colbias

    def v1(kc):
        return jnp.concatenate([vb_ref[keys(kc), :], ones], axis=1)

    def shift(kc):
        return cs * (q0 - kc * tk).astype(F32)

    n_last = (q0 + tq - 1) // tk

    def valid(rows, c):
        r = lax.broadcasted_iota(jnp.int32, (SOFT_ROWS, tk), 0) + rows.start % tq
        col = lax.broadcasted_iota(jnp.int32, (SOFT_ROWS, tk), 1)
        return r - col + (q0 - c * tk) >= 0

    _flash_chunks(n_last, qk, v1, shift, valid, tq == 2 * tk, *scratch)
    acc = scratch[-1][...]

    lv = lam_ref[...]
    lam = (jnp.exp(jnp.sum(lv[0:1] * lv[1:2], axis=1, keepdims=True))
           - jnp.exp(jnp.sum(lv[2:3] * lv[3:4], axis=1, keepdims=True)) + LAM_INIT)
    y = (acc[:tq, :LANES] / acc[:tq, LANES:]) - lam * (acc[tq:, :LANES] / acc[tq:, LANES:])
    ms = jnp.mean(y * y, axis=-1, keepdims=True)
    y = (y * lax.rsqrt(ms + RMS_EPS) * g_ref[...]) * (1.0 - LAM_INIT)
    o_ref[...] = (y * _silu(zb_ref[...].astype(F32))).astype(o_ref.dtype)


def _diff(p, slopes, lamv, g_subln, tq, tk):
    s = p.shape[0]
    nq = s // tq
    assert tq in (tk, 2 * tk) and s % tq == 0
    cb = lambda c: c // LANES
    kern = functools.partial(_diff_kernel, tq=tq, tk=tk)
    return pl.pallas_call(
        kern,
        out_shape=jax.ShapeDtypeStruct((s, WIDTH), CDT),
        grid=(N_HEADS_B, nq),
        in_specs=[
            pl.BlockSpec(memory_space=pltpu.SMEM),
            pl.BlockSpec((4, DIFF_DIM), lambda h, i: (0, 0)),
            pl.BlockSpec((tq, LANES), lambda h, i: (i, cb(C_QB) + h)),
            pl.BlockSpec((s, LANES), lambda h, i: (0, cb(C_KB) + h)),
            pl.BlockSpec((s, LANES), lambda h, i: (0, cb(C_VB) + h)),
            pl.BlockSpec((tq, LANES), lambda h, i: (i, cb(C_ZB) + h)),
            pl.BlockSpec((1, LANES), lambda h, i: (0, 0)),
        ],
        out_specs=pl.BlockSpec((tq, LANES), lambda h, i: (i, h)),
        scratch_shapes=_flash_scratch(2 * tq, tk),
        compiler_params=pltpu.CompilerParams(
            dimension_semantics=("parallel", "parallel"), vmem_limit_bytes=VMEM_LIMIT),
        name="diff",
    )(slopes, lamv, p, p, p, p, g_subln)


def _memkv_kernel(mem_ref, g_ref, w_ref, o_ref, h_ref):
    @pl.when(pl.program_id(0) == 0)
    def _():
        xf = mem_ref[...]
        ms = jnp.mean(xf * xf, axis=-1, keepdims=True)
        h_ref[...] = (xf * lax.rsqrt(ms + RMS_EPS) * g_ref[...]).astype(h_ref.dtype)

    o_ref[...] = jnp.dot(h_ref[...], w_ref[...].astype(h_ref.dtype),
                         preferred_element_type=F32).astype(o_ref.dtype)


def _memkv(mem2, g, w, tn):
    nm, d = mem2.shape
    n = w.shape[1]
    return pl.pallas_call(
        _memkv_kernel,
        out_shape=jax.ShapeDtypeStruct((nm, n), CDT),
        grid=(n // tn,),
        in_specs=[
            pl.BlockSpec((nm, d), lambda j: (0, 0)),
            pl.BlockSpec((1, d), lambda j: (0, 0)),
            pl.BlockSpec((d, tn), lambda j: (0, j)),
        ],
        out_specs=pl.BlockSpec((nm, tn), lambda j: (0, j)),
        scratch_shapes=[pltpu.VMEM((nm, d), CDT)],
        compiler_params=pltpu.CompilerParams(
            dimension_semantics=("arbitrary",), vmem_limit_bytes=VMEM_LIMIT),
        name="memkv",
    )(mem2, g, w)


def _mem_branch(q_ref, z_ref, kv_ref):
    outs = []
    for hh in range(N_MEM_HEADS):
        cols = slice(hh * MEM_HEAD_DIM, (hh + 1) * MEM_HEAD_DIM)
        vcols = slice(WIDTH + hh * MEM_HEAD_DIM, WIDTH + (hh + 1) * MEM_HEAD_DIM)
        s = _nt_dot(q_ref[:, cols], kv_ref[:, cols]) * (MEM_HEAD_DIM ** -0.5)
        m = jnp.max(s, axis=1, keepdims=True)
        p = jnp.exp(s - m)
        l = jnp.sum(p, axis=1, keepdims=True)
        y = jnp.dot(p.astype(kv_ref.dtype), kv_ref[:, vcols], preferred_element_type=F32) / l
        outs.append((y * _silu(z_ref[:, cols].astype(F32))).astype(kv_ref.dtype))
    return jnp.concatenate(outs, axis=1)


def _merge_kernel(ua_ref, ub_ref, qm_ref, zm_ref, mkv_ref, ga_ref, gb_ref, gm_ref, b_ref,
                  wa_ref, wb_ref, wm_ref, o_ref):
    d = o_ref.shape[1]
    um = _mem_branch(qm_ref, zm_ref, mkv_ref)
    acc = None
    for t, (u, g_ref, w_ref) in enumerate(
            ((ua_ref[...], ga_ref, wa_ref), (ub_ref[...], gb_ref, wb_ref), (um, gm_ref, wm_ref))):
        gate = jax.nn.sigmoid(g_ref[...].astype(F32) + b_ref[:, t * d:(t + 1) * d])
        term = gate * jnp.dot(u, w_ref[...], preferred_element_type=F32)
        acc = term if acc is None else acc + term
    o_ref[...] = acc.astype(o_ref.dtype)


def _merge(ua, ub, mkv, p, b_gate, wa, wb, wm, tm):
    s = ua.shape[0]
    d = D_MODEL
    once = pl.Buffered(1)
    u_spec = pl.BlockSpec((tm, WIDTH), lambda i: (i, 0))
    w_spec = pl.BlockSpec((WIDTH, d), lambda i: (0, 0), pipeline_mode=once)
    return pl.pallas_call(
        _merge_kernel,
        out_shape=jax.ShapeDtypeStruct((s, d), CDT),
        grid=(s // tm,),
        in_specs=[
            u_spec, u_spec,
            pl.BlockSpec((tm, WIDTH), lambda i: (i, C_QM // WIDTH)),
            pl.BlockSpec((tm, WIDTH), lambda i: (i, C_ZM // WIDTH)),
            pl.BlockSpec(mkv.shape, lambda i: (0, 0), pipeline_mode=once),
            pl.BlockSpec((tm, d), lambda i: (i, C_GL // d + 0)),
            pl.BlockSpec((tm, d), lambda i: (i, C_GL // d + 1)),
            pl.BlockSpec((tm, d), lambda i: (i, C_GL // d + 2)),
            pl.BlockSpec((1, N_BRANCH * d), lambda i: (0, 0)),
            w_spec, w_spec, w_spec,
        ],
        out_specs=pl.BlockSpec((tm, d), lambda i: (i, 0)),
        compiler_params=pltpu.CompilerParams(
            dimension_semantics=("parallel",), vmem_limit_bytes=VMEM_LIMIT),
        name="merge",
    )(ua, ub, p, p, mkv, p, p, p, b_gate, wa, wb, wm)


def _out_kernel(x_ref, mg_ref, w_ref, g_ref, o_ref):
    y = x_ref[...] + jnp.dot(mg_ref[...], w_ref[...], preferred_element_type=F32)
    ms = jnp.mean(y * y, axis=-1, keepdims=True)
    o_ref[...] = (y * lax.rsqrt(ms + RMS_EPS) * g_ref[...]).astype(o_ref.dtype)


def _out(x2, merged, w_out, g_final, tm):
    s, d = x2.shape
    return pl.pallas_call(
        _out_kernel,
        out_shape=jax.ShapeDtypeStruct((s, d), x2.dtype),
        grid=(s // tm,),
        in_specs=[
            pl.BlockSpec((tm, d), lambda i: (i, 0)),
            pl.BlockSpec((tm, d), lambda i: (i, 0)),
            pl.BlockSpec((d, d), lambda i: (0, 0), pipeline_mode=pl.Buffered(1)),
            pl.BlockSpec((1, d), lambda i: (0, 0)),
        ],
        out_specs=pl.BlockSpec((tm, d), lambda i: (i, 0)),
        compiler_params=pltpu.CompilerParams(
            dimension_semantics=("parallel",), vmem_limit_bytes=VMEM_LIMIT),
        name="out",
    )(x2, merged, w_out, g_final)


def _column_moves():
    src = {}
    o = 0
    for name, width in (("qa", WIDTH), ("ka", WIDTH), ("va", WIDTH), ("za", WIDTH),
                        ("qi", N_IDX_HEADS * IDX_DIM), ("ki", IDX_DIM), ("wi", N_IDX_HEADS),
                        ("qb", WIDTH), ("kb", WIDTH), ("vb", WIDTH), ("zb", WIDTH),
                        ("qm", WIDTH), ("zm", WIDTH), ("gl", N_BRANCH * D_MODEL)):
        src[name] = (o, width)
        o += width
    dst = (("gl", C_GL), ("qi", C_QI), ("qa", C_QA), ("ka", C_KA), ("va", C_VA), ("za", C_ZA),
           ("qb", C_QB), ("kb", C_KB), ("vb", C_VB), ("zb", C_ZB), ("qm", C_QM), ("zm", C_ZM),
           ("ki", C_KK), ("ki", C_KK + IDX_DIM), ("wi", C_WI))
    return o, [(d,) + src[name] for name, d in dst]


WPREP_ROWS = 256
WPREP_ALIGN = 16


def _wprep_kernel(off_ref, w_ref, o_ref):
    del off_ref
    last = pl.num_programs(0) - 1

    @pl.when(pl.program_id(0) != last)
    def _():
        o_ref[...] = w_ref[...].astype(o_ref.dtype)

    @pl.when(pl.program_id(0) == last)
    def _():
        ki = w_ref[:IDX_DIM, :].astype(o_ref.dtype)
        o_ref[:IDX_DIM, :] = ki
        o_ref[IDX_DIM:2 * IDX_DIM, :] = ki
        o_ref[2 * IDX_DIM:2 * IDX_DIM + N_IDX_HEADS, :] = (
            w_ref[IDX_DIM:IDX_DIM + N_IDX_HEADS, :].astype(o_ref.dtype))
        o_ref[2 * IDX_DIM + N_IDX_HEADS:, :] = jnp.zeros(
            (WPREP_ROWS - 2 * IDX_DIM - N_IDX_HEADS, o_ref.shape[1]), o_ref.dtype)


def _reorder_w_in_t(w):
    d_in, moves = _column_moves()
    assert w.shape[1] == d_in and D_PAD - C_KK == WPREP_ROWS
    wt = jnp.swapaxes(w, 0, 1)
    offs = []
    for d, s0, width in sorted(moves):
        if d < C_KK:
            assert width % WPREP_ROWS == 0 and d == len(offs) * WPREP_ROWS
            offs += list(range(s0, s0 + width, WPREP_ROWS))
    offs.append(dict((d, s0) for d, s0, _ in moves)[C_KK])
    assert all(o % WPREP_ALIGN == 0 for o in offs)
    n_d = w.shape[0]
    return pl.pallas_call(
        _wprep_kernel,
        out_shape=jax.ShapeDtypeStruct((D_PAD, n_d), CDT),
        grid_spec=pltpu.PrefetchScalarGridSpec(
            num_scalar_prefetch=1,
            grid=(len(offs),),
            in_specs=[pl.BlockSpec((pl.Element(WPREP_ROWS), pl.Element(n_d)),
                                   lambda i, off: (off[i] * WPREP_ALIGN, 0))],
            out_specs=pl.BlockSpec((WPREP_ROWS, n_d), lambda i, off: (i, 0)),
        ),
        compiler_params=pltpu.CompilerParams(
            dimension_semantics=("parallel",), vmem_limit_bytes=VMEM_LIMIT),
        name="wprep",
    )(jnp.asarray([o // WPREP_ALIGN for o in offs], jnp.int32), wt)


def kernel(x, mem, g_in, w_in, b_gate, lam_q1, lam_k1, lam_q2, lam_k2, g_subln, g_mem,
           w_mem_kv, w_br_a, w_br_b, w_br_m, w_out, g_final):
    bsz, s_len, d = x.shape
    assert bsz == 1 and d == D_MODEL and g_in.shape[0] == 1
    topk = min(TOPK_MAX, s_len // 4)
    x2 = x.reshape(s_len, d)
    slopes = 2.0 ** (-8.0 * jnp.arange(1, N_HEADS_A + 1, dtype=F32) / N_HEADS_A)

    p = _proj(x2, g_in[0].reshape(1, d), _reorder_w_in_t(w_in[0]), tm=1024, tn=768)
    ua = _dsa(p, slopes, tq=512, tk=512, topk=topk)
    lamv = jnp.stack([lam_q1[0], lam_k1[0], lam_q2[0], lam_k2[0]]).astype(F32)
    ub = _diff(p, slopes, lamv, g_subln[0].reshape(1, 2 * DIFF_DIM), tq=512, tk=512)
    mkv = _memkv(mem.reshape(-1, d), g_mem[0].reshape(1, d), w_mem_kv[0], tn=512)
    merged = _merge(ua, ub, mkv, p, b_gate[0].reshape(1, -1),
                    w_br_a[0].astype(CDT), w_br_b[0].astype(CDT), w_br_m[0].astype(CDT), tm=512)
    y = _out(x2, merged, w_out[0].astype(CDT), g_final.reshape(1, d), tm=512)
    return y.reshape(bsz, s_len, d)
```

```python
import functools

import jax
import jax.numpy as jnp
from jax import lax
from jax.experimental import pallas as pl
from jax.experimental.pallas import tpu as pltpu

F32 = jnp.float32
CDT = jnp.bfloat16

D_MODEL = 2048
N_HEADS_A = 8
HEAD_DIM_A = 128
N_IDX_HEADS = 16
IDX_DIM = 64
TOPK_MAX = 256
N_HEADS_B = 8
DIFF_DIM = 64
N_MEM_HEADS = 4
MEM_HEAD_DIM = 256
N_BRANCH = 3
WIDTH = 1024
RMS_EPS = 1e-6
NEG_INF = -1e30
F32_BIG = 3.0e38
LAM_INIT = 0.2
LANES = 128
LOG2E = 1.4426950408889634

C_GL = 0
C_QI = 6144
C_QA = 7168
C_KA = 8192
C_VA = 9216
C_ZA = 10240
C_QB = 11264
C_KB = 12288
C_VB = 13312
C_ZB = 14336
C_QM = 15360
C_ZM = 16384
C_KK = 17408
C_WI = 17536
D_PAD = 17664

VMEM_LIMIT = 56 * 1024 * 1024

BISECT_MAX_ITERS = 96
FLASH_PAIRS_PER_TRIP = 2
NORM_ROWS = 128
SOFT_ROWS = 16


def _nt_dot(a, b):
    return lax.dot_general(a, b, (((1,), (1,)), ((), ())), preferred_element_type=F32)


def _silu(z):
    return z * jax.nn.sigmoid(z)


def _proj_kernel(x_ref, g_ref, w_ref, o_ref, h_ref):
    @pl.when(pl.program_id(1) == 0)
    def _():
        for r in range(0, x_ref.shape[0], NORM_ROWS):
            xf = x_ref[r:r + NORM_ROWS, :]
            ms = jnp.mean(xf * xf, axis=-1, keepdims=True)
            h_ref[r:r + NORM_ROWS, :] = (xf * lax.rsqrt(ms + RMS_EPS) * g_ref[...]).astype(h_ref.dtype)

    o_ref[...] = _nt_dot(h_ref[...], w_ref[...]).astype(o_ref.dtype)


def _proj(x2, g, wt, tm, tn):
    s, d = x2.shape
    n = wt.shape[0]
    return pl.pallas_call(
        _proj_kernel,
        out_shape=jax.ShapeDtypeStruct((s, n), CDT),
        grid=(s // tm, n // tn),
        in_specs=[
            pl.BlockSpec((tm, d), lambda i, j: (i, 0)),
            pl.BlockSpec((1, d), lambda i, j: (0, 0)),
            pl.BlockSpec((tn, d), lambda i, j: (j, 0)),
        ],
        out_specs=pl.BlockSpec((tm, tn), lambda i, j: (i, j)),
        scratch_shapes=[pltpu.VMEM((tm, d), CDT)],
        compiler_params=pltpu.CompilerParams(
            dimension_semantics=("parallel", "arbitrary"), vmem_limit_bytes=VMEM_LIMIT),
        name="proj",
    )(x2, g, wt)


def _tile_lanes(x, n):
    return x if n == 1 else jnp.concatenate([x] * n, axis=1)


def _flash_scratch(rows, tk):
    return [pltpu.VMEM((rows, tk), F32), pltpu.VMEM((rows, tk), F32),
            pltpu.VMEM((rows, tk), CDT), pltpu.VMEM((rows, tk), CDT),
            pltpu.VMEM((rows, LANES), F32), pltpu.VMEM((rows, LANES), F32),
            pltpu.VMEM((rows, LANES), F32), pltpu.VMEM((rows, 2 * LANES), F32)]


def _flash_chunks(n_last, qk, v1, shift, valid, masked_pair, s_a, s_b, p_a, p_b, al_a, al_b, m_ref,
                  acc_ref):
    tk = s_a.shape[1]

    def soft(s_ref, p_ref, al_ref, c, masked=False):
        d = shift(c)
        for g in range(s_ref.shape[0] // SOFT_ROWS):
            rows = slice(g * SOFT_ROWS, (g + 1) * SOFT_ROWS)
            s = s_ref[rows, :]
            if masked and valid is not None:
                s = jnp.where(valid(rows, c), s, NEG_INF)
            m = m_ref[rows, :]
            m_new = jnp.maximum(m, jnp.max(s, axis=1, keepdims=True) - d)
            al_ref[rows, :] = jnp.exp2(m - m_new)
            m_ref[rows, :] = m_new
            p_ref[rows, :] = jnp.exp2(s - _tile_lanes(m_new + d, tk // LANES)).astype(p_ref.dtype)

    def pv(p_ref, al_ref, c):
        acc_ref[...] = (_tile_lanes(al_ref[...], 2) * acc_ref[...]
                        + jnp.dot(p_ref[...], v1(c), preferred_element_type=F32))

    m_ref[...] = jnp.full(m_ref.shape, NEG_INF, F32)
    acc_ref[...] = jnp.zeros(acc_ref.shape, F32)
    s_a[...] = qk(0)

    def pair(c0, lead=True):
        if lead:
            pv(p_b, al_b, c0 - 1)
        s_b[...] = qk(c0 + 1)
        soft(s_a, p_a, al_a, c0)
        pv(p_a, al_a, c0)
        s_a[...] = qk(c0 + 2)
        soft(s_b, p_b, al_b, c0 + 1)

    span = 2 * FLASH_PAIRS_PER_TRIP
    trips = n_last // span
    rest = (n_last % span) // 2

    @pl.when(trips > 0)
    def _():
        pair(0, lead=False)
        for u in range(1, FLASH_PAIRS_PER_TRIP):
            pair(2 * u)

    def multi_body(j, _):
        for u in range(FLASH_PAIRS_PER_TRIP):
            pair(span * j + 2 * u)
        return 0

    lax.fori_loop(1, trips, multi_body, 0)

    @pl.when(jnp.logical_and(trips == 0, rest > 0))
    def _():
        pair(0, lead=False)

    def pair_body(j, _):
        pair(span * trips + 2 * j)
        return 0

    lax.fori_loop(jnp.where(trips == 0, 1, 0), rest, pair_body, 0)

    def odd_tail(lead):
        c0 = n_last - 1
        if lead:
            pv(p_b, al_b, c0 - 1)
        s_b[...] = qk(n_last)
        soft(s_a, p_a, al_a, c0, masked=masked_pair)
        pv(p_a, al_a, c0)
        soft(s_b, p_b, al_b, n_last, masked=True)
        pv(p_b, al_b, n_last)

    def even_tail(lead):
        if lead:
            pv(p_b, al_b, n_last - 1)
        soft(s_a, p_a, al_a, n_last, masked=True)
        pv(p_a, al_a, n_last)

    odd = n_last % 2 == 1
    pl.when(jnp.logical_and(odd, n_last > 1))(functools.partial(odd_tail, True))
    pl.when(n_last == 1)(functools.partial(odd_tail, False))
    if not masked_pair:
        pl.when(jnp.logical_and(jnp.logical_not(odd), n_last > 0))(functools.partial(even_tail, True))
        pl.when(n_last == 0)(functools.partial(even_tail, False))


def _dsa_kernel(slopes_ref, qa_ref, ka_ref, va_ref, za_ref, qi_ref, kk_ref, wi_ref, o_ref,
                sc_ref, qz_ref, wb_ref, tb_ref, cnt_ref, *scratch, tq, tk, topk):
    qi = pl.program_id(0)
    h = pl.program_id(1)
    q0 = qi * tq
    n_full = (q0 + 1) // tk
    row = lax.broadcasted_iota(jnp.int32, (tq, tk), 0)
    col = lax.broadcasted_iota(jnp.int32, (tq, tk), 1)
    rc = row - col
    nsub = tk // LANES
    rblk = LANES

    def keys(kc):
        return pl.ds(pl.multiple_of(kc * tk, tk), tk)

    def fold(a, op):
        r = a[:, :LANES]
        for t in range(1, nsub):
            r = op(r, a[:, t * LANES:(t + 1) * LANES])
        return r

    @pl.when(h == 0)
    def _select():
        lane = lax.broadcasted_iota(jnp.int32, (tq, LANES), 1)
        for m in range(N_IDX_HEADS // 2):
            qp = qi_ref[:, m * LANES:(m + 1) * LANES].astype(F32)
            qz_ref[2 * m] = jnp.where(lane < IDX_DIM, qp, 0.0).astype(qz_ref.dtype)
            qz_ref[2 * m + 1] = jnp.where(lane >= IDX_DIM, qp, 0.0).astype(qz_ref.dtype)
        wi = wi_ref[...].astype(F32) * (IDX_DIM ** -0.5 * N_IDX_HEADS ** -0.5)
        for j in range(N_IDX_HEADS):
            wb_ref[j] = jnp.broadcast_to(wi[:, j:j + 1], (tq, LANES))

        def chunk_scores(kc):
            kk = kk_ref[keys(kc), :]
            acc = jnp.zeros((tq, tk), F32)
            for j in range(N_IDX_HEADS):
                acc = acc + jnp.maximum(_nt_dot(qz_ref[j], kk), 0.0) * _tile_lanes(wb_ref[j], nsub)
            return acc

        def full_body(kc, carry):
            rmin, rmax = carry
            a = chunk_scores(kc)
            sc_ref[kc] = a
            return jnp.minimum(rmin, fold(a, jnp.minimum)), jnp.maximum(rmax, fold(a, jnp.maximum))

        def full_pair(j, carry):
            c0 = n_full % 2 + 2 * j
            return full_body(c0 + 1, full_body(c0, carry))

        carry = lax.fori_loop(
            0, n_full % 2, full_body,
            (jnp.full((tq, LANES), F32_BIG, F32), jnp.full((tq, LANES), -F32_BIG, F32)))
        rmin, rmax = lax.fori_loop(0, n_full // 2, full_pair, carry)
        a = chunk_scores(n_full)
        causal = rc + (q0 - n_full * tk) >= 0
        sc_ref[n_full] = jnp.where(causal, a, -F32_BIG)
        rmin = jnp.minimum(rmin, fold(jnp.where(causal, a, F32_BIG), jnp.minimum))
        rmax = jnp.maximum(rmax, fold(jnp.where(causal, a, -F32_BIG), jnp.maximum))
        nblk = tq // rblk
        sub = lax.broadcasted_iota(jnp.int32, (8, LANES), 0)
        lane_id = lax.broadcasted_iota(jnp.int32, (8, LANES), 1)
        pad = [jnp.zeros((8 - nblk, LANES), F32)]

        def compact(x, reduce):
            return jnp.concatenate(
                [reduce(x[r * rblk:(r + 1) * rblk].T, axis=0, keepdims=True) for r in range(nblk)] + pad,
                axis=0)

        def expand(c):
            return jnp.concatenate(
                [jnp.broadcast_to(c[r:r + 1, :], (rblk, LANES)).T for r in range(nblk)], axis=0)

        sc_ref[n_full + 1] = jnp.full((tq, tk), -F32_BIG, F32)

        def count(tc, strict=False):
            tb_ref[...] = expand(tc)
            cnt_ref[...] = jnp.zeros(cnt_ref.shape, F32)

            def cbody(j, _):
                for g in range(tq // 8):
                    rows = slice(g * 8, (g + 1) * 8)
                    t = tb_ref[rows, :]
                    c = cnt_ref[rows, :]
                    for u in range(2):
                        x = sc_ref[2 * j + u, rows, :]
                        for l in range(nsub):
                            xl = x[:, l * LANES:(l + 1) * LANES]
                            c = c + jnp.where(xl > t if strict else xl >= t, 1.0, 0.0)
                    cnt_ref[rows, :] = c
                return 0

            lax.fori_loop(0, (n_full + 2) // 2, cbody, 0)
            return compact(cnt_ref[...], jnp.sum)

        kf = float(topk)
        n_causal = jnp.where(sub < nblk, q0 + sub * rblk + lane_id + 1, 0).astype(F32)
        lo0 = compact(rmin, jnp.min)
        hi0 = compact(rmax, jnp.max)
        hi0 = hi0 + (jnp.abs(hi0) * 1e-6 + 1e-30)

        def cond(c):
            return jnp.logical_and(c[0] < BISECT_MAX_ITERS, c[4])

        def body(c):
            it, lo, hi, cnt_lo, _ = c
            mid = lo + 0.5 * (hi - lo)
            cm = count(mid)
            ge = cm >= kf
            moving = jnp.logical_and(mid > lo, mid < hi)
            lo = jnp.where(ge, mid, lo)
            hi = jnp.where(ge, hi, mid)
            cnt_lo = jnp.where(ge, cm, cnt_lo)
            active = jnp.logical_and(cnt_lo > kf, moving)
            flag = jnp.max(jnp.where(active, 1.0, 0.0)) > 0.0
            return it + 1, lo, hi, cnt_lo, flag

        c0 = (jnp.int32(0), lo0, hi0, n_causal, jnp.max(n_causal) > kf)
        _, lo, _, cnt_lo, _ = lax.while_loop(cond, body, c0)
        lob = _tile_lanes(expand(lo), nsub)
        tied = jnp.max(cnt_lo) > kf

        @pl.when(jnp.logical_not(tied))
        def _():
            def mbody(kc, _):
                sc_ref[kc] = jnp.where(sc_ref[kc] >= lob, 0.0, NEG_INF)
                return 0

            lax.fori_loop(0, n_full + 1, mbody, 0)

        @pl.when(tied)
        def _():
            need = _tile_lanes(expand(kf - count(lo, strict=True)), nsub)
            ki = lax.broadcasted_iota(jnp.int32, (tk, tk), 0)
            kj = lax.broadcasted_iota(jnp.int32, (tk, tk), 1)
            prefix_ones = jnp.where(ki <= kj, 1.0, 0.0).astype(CDT)
            chunk_ones = jnp.ones((tk, LANES), CDT)

            def tbody(kc, seen):
                sc = sc_ref[kc]
                eq = sc == lob
                eqf = jnp.where(eq, 1.0, 0.0).astype(CDT)
                rank = jnp.dot(eqf, prefix_ones, preferred_element_type=F32) + _tile_lanes(seen, nsub)
                keep = jnp.logical_or(sc > lob, jnp.logical_and(eq, rank <= need))
                sc_ref[kc] = jnp.where(keep, 0.0, NEG_INF)
                return seen + jnp.dot(eqf, chunk_ones, preferred_element_type=F32)

            lax.fori_loop(0, n_full + 1, tbody, jnp.zeros((tq, LANES), F32))

    cs = slopes_ref[h] * LOG2E
    q = (qa_ref[...].astype(F32) * (HEAD_DIM_A ** -0.5 * LOG2E)).astype(qa_ref.dtype)
    colbias = cs * lax.broadcasted_iota(jnp.int32, (1, tk), 1).astype(F32)
    ones = jnp.ones((tk, LANES), va_ref.dtype)

    def qk(kc):
        return _nt_dot(q, ka_ref[keys(kc), :]) + colbias + sc_ref[kc]

    def v1(kc):
        return jnp.concatenate([va_ref[keys(kc), :], ones], axis=1)

    def shift(kc):
        return cs * (q0 - kc * tk).astype(F32)

    _flash_chunks(n_full, qk, v1, shift, None, False, *scratch)
    acc = scratch[-1][...]
    y = acc[:, :LANES] / acc[:, LANES:]
    o_ref[...] = (y * _silu(za_ref[...].astype(F32))).astype(o_ref.dtype)


def _dsa(p, slopes, tq, tk, topk):
    s = p.shape[0]
    nq = s // tq
    assert tk >= tq and s % tk == 0
    cb = lambda c: c // LANES
    kern = functools.partial(_dsa_kernel, tq=tq, tk=tk, topk=topk)
    return pl.pallas_call(
        kern,
        out_shape=jax.ShapeDtypeStruct((s, WIDTH), CDT),
        grid=(nq, N_HEADS_A),
        in_specs=[
            pl.BlockSpec(memory_space=pltpu.SMEM),
            pl.BlockSpec((tq, LANES), lambda i, h: (i, cb(C_QA) + h)),
            pl.BlockSpec((s, LANES), lambda i, h: (0, cb(C_KA) + h)),
            pl.BlockSpec((s, LANES), lambda i, h: (0, cb(C_VA) + h)),
            pl.BlockSpec((tq, LANES), lambda i, h: (i, cb(C_ZA) + h)),
            pl.BlockSpec((tq, WIDTH), lambda i, h: (i, C_QI // WIDTH)),
            pl.BlockSpec((s, LANES), lambda i, h: (0, cb(C_KK))),
            pl.BlockSpec((tq, LANES), lambda i, h: (i, cb(C_WI))),
        ],
        out_specs=pl.BlockSpec((tq, LANES), lambda i, h: (i, h)),
        scratch_shapes=[
            pltpu.VMEM((s // tk + 1, tq, tk), F32),
            pltpu.VMEM((N_IDX_HEADS, tq, LANES), CDT),
            pltpu.VMEM((N_IDX_HEADS, tq, LANES), F32),
            pltpu.VMEM((tq, LANES), F32),
            pltpu.VMEM((tq, LANES), F32),
        ] + _flash_scratch(tq, tk),
        compiler_params=pltpu.CompilerParams(
            dimension_semantics=("parallel", "arbitrary"), vmem_limit_bytes=VMEM_LIMIT),
        name="dsa",
    )(slopes, p, p, p, p, p, p, p)


def _diff_kernel(slopes_ref, lam_ref, qb_ref, kb_ref, vb_ref, zb_ref, g_ref, o_ref, *scratch, tq, tk):
    h = pl.program_id(0)
    qi = pl.program_id(1)
    q0 = qi * tq
    cs = slopes_ref[h] * LOG2E
    lane = lax.broadcasted_iota(jnp.int32, (tq, LANES), 1)
    qf = qb_ref[...].astype(F32) * (DIFF_DIM ** -0.5 * LOG2E)
    qs = jnp.concatenate([jnp.where(lane < DIFF_DIM, qf, 0.0),
                          jnp.where(lane >= DIFF_DIM, qf, 0.0)], axis=0).astype(qb_ref.dtype)
    colbias = cs * lax.broadcasted_iota(jnp.int32, (1, tk), 1).astype(F32)
    ones = jnp.ones((tk, LANES), vb_ref.dtype)

    def keys(kc):
        return pl.ds(pl.multiple_of(kc * tk, tk), tk)

    def qk(kc):
        return _nt_dot(qs, kb_ref[keys(kc), :]) + colbias

    def v1(kc):
        return jnp.concatenate([vb_ref[keys(kc), :], ones], axis=1)

    def shift(kc):
        return cs * (q0 - kc * tk).astype(F32)

    n_last = (q0 + tq - 1) // tk

    def valid(rows, c):
        r = lax.broadcasted_iota(jnp.int32, (SOFT_ROWS, tk), 0) + rows.start % tq
        col = lax.broadcasted_iota(jnp.int32, (SOFT_ROWS, tk), 1)
        return r - col + (q0 - c * tk) >= 0

    _flash_chunks(n_last, qk, v1, shift, valid, tq == 2 * tk, *scratch)
    acc = scratch[-1][...]

    lv = lam_ref[...]
    lam = (jnp.exp(jnp.sum(lv[0:1] * lv[1:2], axis=1, keepdims=True))
           - jnp.exp(jnp.sum(lv[2:3] * lv[3:4], axis=1, keepdims=True)) + LAM_INIT)
    y = (acc[:tq, :LANES] / acc[:tq, LANES:]) - lam * (acc[tq:, :LANES] / acc[tq:, LANES:])
    ms = jnp.mean(y * y, axis=-1, keepdims=True)
    y = (y * lax.rsqrt(ms + RMS_EPS) * g_ref[...]) * (1.0 - LAM_INIT)
    o_ref[...] = (y * _silu(zb_ref[...].astype(F32))).astype(o_ref.dtype)


def _diff(p, slopes, lamv, g_subln, tq, tk):
    s = p.shape[0]
    nq = s // tq
    assert tq in (tk, 2 * tk) and s % tq == 0
    cb = lambda c: c // LANES
    kern = functools.partial(_diff_kernel, tq=tq, tk=tk)
    return pl.pallas_call(
        kern,
        out_shape=jax.ShapeDtypeStruct((s, WIDTH), CDT),
        grid=(N_HEADS_B, nq),
        in_specs=[
            pl.BlockSpec(memory_space=pltpu.SMEM),
            pl.BlockSpec((4, DIFF_DIM), lambda h, i: (0, 0)),
            pl.BlockSpec((tq, LANES), lambda h, i: (i, cb(C_QB) + h)),
            pl.BlockSpec((s, LANES), lambda h, i: (0, cb(C_KB) + h)),
            pl.BlockSpec((s, LANES), lambda h, i: (0, cb(C_VB) + h)),
            pl.BlockSpec((tq, LANES), lambda h, i: (i, cb(C_ZB) + h)),
            pl.BlockSpec((1, LANES), lambda h, i: (0, 0)),
        ],
        out_specs=pl.BlockSpec((tq, LANES), lambda h, i: (i, h)),
        scratch_shapes=_flash_scratch(2 * tq, tk),
        compiler_params=pltpu.CompilerParams(
            dimension_semantics=("parallel", "parallel"), vmem_limit_bytes=VMEM_LIMIT),
        name="diff",
    )(slopes, lamv, p, p, p, p, g_subln)


def _memkv_kernel(mem_ref, g_ref, w_ref, o_ref, h_ref):
    @pl.when(pl.program_id(0) == 0)
    def _():
        xf = mem_ref[...]
        ms = jnp.mean(xf * xf, axis=-1, keepdims=True)
        h_ref[...] = (xf * lax.rsqrt(ms + RMS_EPS) * g_ref[...]).astype(h_ref.dtype)

    o_ref[...] = jnp.dot(h_ref[...], w_ref[...].astype(h_ref.dtype),
                         preferred_element_type=F32).astype(o_ref.dtype)


def _memkv(mem2, g, w, tn):
    nm, d = mem2.shape
    n = w.shape[1]
    return pl.pallas_call(
        _memkv_kernel,
        out_shape=jax.ShapeDtypeStruct((nm, n), CDT),
        grid=(n // tn,),
        in_specs=[
            pl.BlockSpec((nm, d), lambda j: (0, 0)),
            pl.BlockSpec((1, d), lambda j: (0, 0)),
            pl.BlockSpec((d, tn), lambda j: (0, j)),
        ],
        out_specs=pl.BlockSpec((nm, tn), lambda j: (0, j)),
        scratch_shapes=[pltpu.VMEM((nm, d), CDT)],
        compiler_params=pltpu.CompilerParams(
            dimension_semantics=("arbitrary",), vmem_limit_bytes=VMEM_LIMIT),
        name="memkv",
    )(mem2, g, w)


def _mem_branch(q_ref, z_ref, kv_ref):
    outs = []
    for hh in range(N_MEM_HEADS):
        cols = slice(hh * MEM_HEAD_DIM, (hh + 1) * MEM_HEAD_DIM)
        vcols = slice(WIDTH + hh * MEM_HEAD_DIM, WIDTH + (hh + 1) * MEM_HEAD_DIM)
        s = _nt_dot(q_ref[:, cols], kv_ref[:, cols]) * (MEM_HEAD_DIM ** -0.5)
        m = jnp.max(s, axis=1, keepdims=True)
        p = jnp.exp(s - m)
        l = jnp.sum(p, axis=1, keepdims=True)
        y = jnp.dot(p.astype(kv_ref.dtype), kv_ref[:, vcols], preferred_element_type=F32) / l
        outs.append((y * _silu(z_ref[:, cols].astype(F32))).astype(kv_ref.dtype))
    return jnp.concatenate(outs, axis=1)


def _merge_kernel(ua_ref, ub_ref, qm_ref, zm_ref, mkv_ref, ga_ref, gb_ref, gm_ref, b_ref,
                  wa_ref, wb_ref, wm_ref, o_ref):
    d = o_ref.shape[1]
    um = _mem_branch(qm_ref, zm_ref, mkv_ref)
    acc = None
    for t, (u, g_ref, w_ref) in enumerate(
            ((ua_ref[...], ga_ref, wa_ref), (ub_ref[...], gb_ref, wb_ref), (um, gm_ref, wm_ref))):
        gate = jax.nn.sigmoid(g_ref[...].astype(F32) + b_ref[:, t * d:(t + 1) * d])
        term = gate * jnp.dot(u, w_ref[...], preferred_element_type=F32)
        acc = term if acc is None else acc + term
    o_ref[...] = acc.astype(o_ref.dtype)


def _merge(ua, ub, mkv, p, b_gate, wa, wb, wm, tm):
    s = ua.shape[0]
    d = D_MODEL
    once = pl.Buffered(1)
    u_spec = pl.BlockSpec((tm, WIDTH), lambda i: (i, 0))
    w_spec = pl.BlockSpec((WIDTH, d), lambda i: (0, 0), pipeline_mode=once)
    return pl.pallas_call(
        _merge_kernel,
        out_shape=jax.ShapeDtypeStruct((s, d), CDT),
        grid=(s // tm,),
        in_specs=[
            u_spec, u_spec,
            pl.BlockSpec((tm, WIDTH), lambda i: (i, C_QM // WIDTH)),
            pl.BlockSpec((tm, WIDTH), lambda i: (i, C_ZM // WIDTH)),
            pl.BlockSpec(mkv.shape, lambda i: (0, 0), pipeline_mode=once),
            pl.BlockSpec((tm, d), lambda i: (i, C_GL // d + 0)),
            pl.BlockSpec((tm, d), lambda i: (i, C_GL // d + 1)),
            pl.BlockSpec((tm, d), lambda i: (i, C_GL // d + 2)),
            pl.BlockSpec((1, N_BRANCH * d), lambda i: (0, 0)),
            w_spec, w_spec, w_spec,
        ],
        out_specs=pl.BlockSpec((tm, d), lambda i: (i, 0)),
        compiler_params=pltpu.CompilerParams(
            dimension_semantics=("parallel",), vmem_limit_bytes=VMEM_LIMIT),
        name="merge",
    )(ua, ub, p, p, mkv, p, p, p, b_gate, wa, wb, wm)


def _out_kernel(x_ref, mg_ref, w_ref, g_ref, o_ref):
    y = x_ref[...] + jnp.dot(mg_ref[...], w_ref[...], preferred_element_type=F32)
    ms = jnp.mean(y * y, axis=-1, keepdims=True)
    o_ref[...] = (y * lax.rsqrt(ms + RMS_EPS) * g_ref[...]).astype(o_ref.dtype)


def _out(x2, merged, w_out, g_final, tm):
    s, d = x2.shape
    return pl.pallas_call(
        _out_kernel,
        out_shape=jax.ShapeDtypeStruct((s, d), x2.dtype),
        grid=(s // tm,),
        in_specs=[
            pl.BlockSpec((tm, d), lambda i: (i, 0)),
            pl.BlockSpec((tm, d), lambda i: (i, 0)),
            pl.BlockSpec((d, d), lambda i: (0, 0), pipeline_mode=pl.Buffered(1)),
            pl.BlockSpec((1, d), lambda i: (0, 0)),
        ],
        out_specs=pl.BlockSpec((tm, d), lambda i: (i, 0)),
        compiler_params=pltpu.CompilerParams(
            dimension_semantics=("parallel",), vmem_limit_bytes=VMEM_LIMIT),
        name="out",
    )(x2, merged, w_out, g_final)


def _column_moves():
    src = {}
    o = 0
    for name, width in (("qa", WIDTH), ("ka", WIDTH), ("va", WIDTH), ("za", WIDTH),
                        ("qi", N_IDX_HEADS * IDX_DIM), ("ki", IDX_DIM), ("wi", N_IDX_HEADS),
                        ("qb", WIDTH), ("kb", WIDTH), ("vb", WIDTH), ("zb", WIDTH),
                        ("qm", WIDTH), ("zm", WIDTH), ("gl", N_BRANCH * D_MODEL)):
        src[name] = (o, width)
        o += width
    dst = (("gl", C_GL), ("qi", C_QI), ("qa", C_QA), ("ka", C_KA), ("va", C_VA), ("za", C_ZA),
           ("qb", C_QB), ("kb", C_KB), ("vb", C_VB), ("zb", C_ZB), ("qm", C_QM), ("zm", C_ZM),
           ("ki", C_KK), ("ki", C_KK + IDX_DIM), ("wi", C_WI))
    return o, [(d,) + src[name] for name, d in dst]


WPREP_ROWS = 256
WPREP_ALIGN = 16


def _wprep_kernel(off_ref, w_ref, o_ref):
    del off_ref
    last = pl.num_programs(0) - 1

    @pl.when(pl.program_id(0) != last)
    def _():
        o_ref[...] = w_ref[...].astype(o_ref.dtype)

    @pl.when(pl.program_id(0) == last)
    def _():
        ki = w_ref[:IDX_DIM, :].astype(o_ref.dtype)
        o_ref[:IDX_DIM, :] = ki
        o_ref[IDX_DIM:2 * IDX_DIM, :] = ki
        o_ref[2 * IDX_DIM:2 * IDX_DIM + N_IDX_HEADS, :] = (
            w_ref[IDX_DIM:IDX_DIM + N_IDX_HEADS, :].astype(o_ref.dtype))
        o_ref[2 * IDX_DIM + N_IDX_HEADS:, :] = jnp.zeros(
            (WPREP_ROWS - 2 * IDX_DIM - N_IDX_HEADS, o_ref.shape[1]), o_ref.dtype)


def _reorder_w_in_t(w):
    d_in, moves = _column_moves()
    assert w.shape[1] == d_in and D_PAD - C_KK == WPREP_ROWS
    wt = jnp.swapaxes(w, 0, 1)
    offs = []
    for d, s0, width in sorted(moves):
        if d < C_KK:
            assert width % WPREP_ROWS == 0 and d == len(offs) * WPREP_ROWS
            offs += list(range(s0, s0 + width, WPREP_ROWS))
    offs.append(dict((d, s0) for d, s0, _ in moves)[C_KK])
    assert all(o % WPREP_ALIGN == 0 for o in offs)
    n_d = w.shape[0]
    return pl.pallas_call(
        _wprep_kernel,
        out_shape=jax.ShapeDtypeStruct((D_PAD, n_d), CDT),
        grid_spec=pltpu.PrefetchScalarGridSpec(
            num_scalar_prefetch=1,
            grid=(len(offs),),
            in_specs=[pl.BlockSpec((pl.Element(WPREP_ROWS), pl.Element(n_d)),
                                   lambda i, off: (off[i] * WPREP_ALIGN, 0))],
            out_specs=pl.BlockSpec((WPREP_ROWS, n_d), lambda i, off: (i, 0)),
        ),
        compiler_params=pltpu.CompilerParams(
            dimension_semantics=("parallel",), vmem_limit_bytes=VMEM_LIMIT),
        name="wprep",
    )(jnp.asarray([o // WPREP_ALIGN for o in offs], jnp.int32), wt)


def kernel(x, mem, g_in, w_in, b_gate, lam_q1, lam_k1, lam_q2, lam_k2, g_subln, g_mem,
           w_mem_kv, w_br_a, w_br_b, w_br_m, w_out, g_final):
    bsz, s_len, d = x.shape
    assert bsz == 1 and d == D_MODEL and g_in.shape[0] == 1
    topk = min(TOPK_MAX, s_len // 4)
    x2 = x.reshape(s_len, d)
    slopes = 2.0 ** (-8.0 * jnp.arange(1, N_HEADS_A + 1, dtype=F32) / N_HEADS_A)

    p = _proj(x2, g_in[0].reshape(1, d), _reorder_w_in_t(w_in[0]), tm=1024, tn=768)
    ua = _dsa(p, slopes, tq=512, tk=512, topk=topk)
    lamv = jnp.stack([lam_q1[0], lam_k1[0], lam_q2[0], lam_k2[0]]).astype(F32)
    ub = _diff(p, slopes, lamv, g_subln[0].reshape(1, 2 * DIFF_DIM), tq=512, tk=512)
    mkv = _memkv(mem.reshape(-1, d), g_mem[0].reshape(1, d), w_mem_kv[0], tn=512)
    merged = _merge(ua, ub, mkv, p, b_gate[0].reshape(1, -1),
                    w_br_a[0].astype(CDT), w_br_b[0].astype(CDT), w_br_m[0].astype(CDT), tm=512)
    y = _out(x2, merged, w_out[0].astype(CDT), g_final.reshape(1, d), tm=512)
    return y.reshape(bsz, s_len, d)
```
